```python
import math
import jax, jax.numpy as jnp
from jax import lax
import numpy as np

D_MODEL = 1024
BATCH = 8
SEQ = 4096
DEPTH = 1
DEC_BATCH = 2
DEC_SEQ = 8192
PAST_LEN = 128

D_MIX = 2 * D_MODEL
D_FOURIER = D_MIX // 4
N_FGROUPS = 8
F_GROUP = D_FOURIER // N_FGROUPS
D_SSM = D_MIX - D_FOURIER
HEAD_DIM = 64
N_HEADS = D_SSM // HEAD_DIM
N_BC_GROUPS = 4
HEADS_PER_GROUP = N_HEADS // N_BC_GROUPS
D_STATE = 128
D_CONV = 5
CONV_PAD = D_CONV // 2
CONV_DIM = D_SSM + 2 * N_BC_GROUPS * D_STATE
CHUNK = 128
D_PLE = 256
D_IN_PROJ = 2 * D_FOURIER + D_SSM + CONV_DIM + 2 * N_HEADS
EPS = 1e-6

kernel_name = 'hybrid_fnet_ssd_bidir_encoder'


def rms_norm(x, w):
    xf = x.astype(jnp.float32)
    y = xf * lax.rsqrt(jnp.mean(xf * xf, axis=-1, keepdims=True) + EPS)
    return (y * w.astype(jnp.float32)).astype(x.dtype)


def ssd_chunked(x, dt, A, Bm, Cm):
    b, l, g, r, p = x.shape
    n = Bm.shape[-1]
    c = l // CHUNK
    X = (x * dt[..., None]).reshape(b, c, CHUNK, g, r, p)
    a = (dt * A).reshape(b, c, CHUNK, g, r)
    Bc = Bm.reshape(b, c, CHUNK, g, n)
    Cc = Cm.reshape(b, c, CHUNK, g, n)
    a_cs = jnp.cumsum(a, axis=2)
    seg = a_cs[:, :, :, None] - a_cs[:, :, None, :]
    mask = jnp.tril(jnp.ones((CHUNK, CHUNK), dtype=bool))[:, :, None, None]
    decay = jnp.exp(jnp.where(mask, seg, -jnp.inf))
    cb = jnp.einsum('bclgn,bcsgn->bclsg', Cc, Bc)
    y_diag = jnp.einsum('bclsg,bclsgr,bcsgrp->bclgrp', cb, decay, X)
    decay_st = jnp.exp(a_cs[:, :, -1:] - a_cs)
    states = jnp.einsum('bcsgn,bcsgr,bcsgrp->bcgrpn', Bc, decay_st, X)
    chunk_decay = jnp.exp(a_cs[:, :, -1])

    def step(h, inp):
        s, d = inp
        return h * d[..., None, None] + s, h

    h0 = jnp.zeros((b, g, r, p, n), dtype=jnp.float32)
    _, prev = lax.scan(step, h0, (jnp.moveaxis(states, 1, 0), jnp.moveaxis(chunk_decay, 1, 0)))
    prev = jnp.moveaxis(prev, 0, 1)
    y_off = jnp.einsum('bclgn,bcgrpn,bclgr->bclgrp', Cc, prev, jnp.exp(a_cs))
    return (y_diag + y_off).reshape(b, l, g, r, p)


def mixer_layer(h, p, norm_w, w_in, w_fmix, conv_w, conv_b, a_log_f, a_log_b,
                dt_bias_f, dt_bias_b, d_skip, ssd_norm_w, w_out, w_ple_in, w_ple_gate):
    b, s, _ = h.shape
    f32 = jnp.float32
    u = rms_norm(h, norm_w)
    proj = u @ w_in
    z_f, u_f, z_s, xbc, dt_raw = jnp.split(
        proj, [D_FOURIER, 2 * D_FOURIER, 2 * D_FOURIER + D_SSM,
               2 * D_FOURIER + D_SSM + CONV_DIM], axis=-1)

    uf = u_f.astype(f32).reshape(b, s, N_FGROUPS, F_GROUP)
    four = jnp.fft.fft2(uf, axes=(1, 3), norm='ortho').real
    y_f = jnp.einsum('bsgc,gcd->bsgd', four, w_fmix.astype(f32)).reshape(b, s, D_FOURIER)
    y_f = (y_f * jax.nn.silu(z_f.astype(f32))).astype(h.dtype)

    xbc = lax.conv_general_dilated(
        xbc, conv_w[:, None, :].astype(xbc.dtype), window_strides=(1,),
        padding=[(CONV_PAD, CONV_PAD)], dimension_numbers=('NWC', 'WIO', 'NWC'),
        feature_group_count=CONV_DIM)
    xbc = jax.nn.silu(xbc.astype(f32) + conv_b.astype(f32))
    xs, Bm, Cm = jnp.split(xbc, [D_SSM, D_SSM + N_BC_GROUPS * D_STATE], axis=-1)
    xs = xs.reshape(b, s, N_BC_GROUPS, HEADS_PER_GROUP, HEAD_DIM)
    Bm = Bm.reshape(b, s, N_BC_GROUPS, D_STATE)
    Cm = Cm.reshape(b, s, N_BC_GROUPS, D_STATE)
    dt_f_raw, dt_b_raw = jnp.split(dt_raw.astype(f32), [N_HEADS], axis=-1)
    dt_f = jax.nn.softplus(dt_f_raw + dt_bias_f.astype(f32)).reshape(b, s, N_BC_GROUPS, HEADS_PER_GROUP)
    dt_b = jax.nn.softplus(dt_b_raw + dt_bias_b.astype(f32)).reshape(b, s, N_BC_GROUPS, HEADS_PER_GROUP)
    A_f = -jnp.exp(a_log_f.astype(f32)).reshape(N_BC_GROUPS, HEADS_PER_GROUP)
    A_b = -jnp.exp(a_log_b.astype(f32)).reshape(N_BC_GROUPS, HEADS_PER_GROUP)
    flip = lambda t: jnp.flip(t, axis=1)
    y_fwd = ssd_chunked(xs, dt_f, A_f, Bm, Cm)
    y_bwd = flip(ssd_chunked(flip(xs), flip(dt_b), A_b, flip(Bm), flip(Cm)))
    D = d_skip.astype(f32).reshape(N_BC_GROUPS, HEADS_PER_GROUP)[..., None]
    y_s = (y_fwd + y_bwd + D * xs).reshape(b, s, D_SSM)
    y_s = y_s * jax.nn.silu(z_s.astype(f32))
    yg = y_s.reshape(b, s, N_BC_GROUPS, D_SSM // N_BC_GROUPS)
    yg = yg * lax.rsqrt(jnp.mean(yg * yg, axis=-1, keepdims=True) + EPS)
    y_s = (yg.reshape(b, s, D_SSM) * ssd_norm_w.astype(f32)).astype(h.dtype)

    h = h + jnp.concatenate([y_f, y_s], axis=-1) @ w_out

    gate = jax.nn.sigmoid((h @ w_ple_gate).astype(f32))
    h = h + ((p @ w_ple_in).astype(f32) * gate).astype(h.dtype)
    return h


def encoder_trunk(x, p, norm_w, w_in, w_fmix, conv_w, conv_b, a_log_f, a_log_b,
                  dt_bias_f, dt_bias_b, d_skip, ssd_norm_w, w_out, w_ple_in, w_ple_gate,
                  final_norm_w):
    h = x
    for i in range(DEPTH):
        h = mixer_layer(h, p[i], norm_w[i], w_in[i], w_fmix[i], conv_w[i], conv_b[i],
                        a_log_f[i], a_log_b[i], dt_bias_f[i], dt_bias_b[i], d_skip[i],
                        ssd_norm_w[i], w_out[i], w_ple_in[i], w_ple_gate[i])
    return rms_norm(h, final_norm_w)


def _dt_bias(k):
    dt = jnp.exp(jax.random.uniform(k, (DEPTH, N_HEADS)) * (math.log(0.1) - math.log(0.001)) + math.log(0.001))
    return dt + jnp.log(-jnp.expm1(-dt))


def setup_inputs(seed: int = 0) -> dict:
    key = jax.random.key(seed)
    ks = jax.random.split(key, 20)
    nrm = jax.random.normal
    return {
        'x_prompt': nrm(ks[0], (BATCH, SEQ, D_MODEL), jnp.float32),
        'x_sample': nrm(ks[1], (DEC_BATCH, DEC_SEQ, D_MODEL), jnp.float32),
        'p_prompt': nrm(ks[2], (DEPTH, BATCH, SEQ, D_PLE), jnp.float32),
        'p_sample': nrm(ks[3], (DEPTH, DEC_BATCH, DEC_SEQ, D_PLE), jnp.float32),
        'norm_w': 1.0 + 0.02 * nrm(ks[4], (DEPTH, D_MODEL), jnp.float32),
        'w_in': nrm(ks[5], (DEPTH, D_MODEL, D_IN_PROJ), jnp.float32) * D_MODEL ** -0.5,
        'w_fmix': nrm(ks[6], (DEPTH, N_FGROUPS, F_GROUP, F_GROUP), jnp.float32) * F_GROUP ** -0.5,
        'conv_w': nrm(ks[7], (DEPTH, D_CONV, CONV_DIM), jnp.float32) * D_CONV ** -0.5,
        'conv_b': 0.02 * nrm(ks[8], (DEPTH, CONV_DIM), jnp.float32),
        'a_log_f': jnp.log(jax.random.uniform(ks[9], (DEPTH, N_HEADS), jnp.float32, 1.0, 16.0)),
        'a_log_b': jnp.log(jax.random.uniform(ks[10], (DEPTH, N_HEADS), jnp.float32, 1.0, 16.0)),
        'dt_bias_f': _dt_bias(ks[11]),
        'dt_bias_b': _dt_bias(ks[12]),
        'd_skip': 1.0 + 0.1 * nrm(ks[13], (DEPTH, N_HEADS), jnp.float32),
        'ssd_norm_w': 1.0 + 0.02 * nrm(ks[14], (DEPTH, D_SSM), jnp.float32),
        'w_out': nrm(ks[15], (DEPTH, D_MIX, D_MODEL), jnp.float32) * D_MIX ** -0.5,
        'w_ple_in': nrm(ks[16], (DEPTH, D_PLE, D_MODEL), jnp.float32) * D_PLE ** -0.5,
        'w_ple_gate': nrm(ks[17], (DEPTH, D_MODEL, D_MODEL), jnp.float32) * D_MODEL ** -0.5,
        'final_norm_w': 1.0 + 0.02 * nrm(ks[18], (D_MODEL,), jnp.float32),
    }


def reference(x_prompt, x_sample, p_prompt, p_sample, norm_w, w_in, w_fmix, conv_w, conv_b,
              a_log_f, a_log_b, dt_bias_f, dt_bias_b, d_skip, ssd_norm_w, w_out,
              w_ple_in, w_ple_gate, final_norm_w):
    y_prompt = encoder_trunk(x_prompt, p_prompt, norm_w, w_in, w_fmix, conv_w, conv_b,
                             a_log_f, a_log_b, dt_bias_f, dt_bias_b, d_skip, ssd_norm_w,
                             w_out, w_ple_in, w_ple_gate, final_norm_w)
    y_sample = encoder_trunk(x_sample, p_sample, norm_w, w_in, w_fmix, conv_w, conv_b,
                             a_log_f, a_log_b, dt_bias_f, dt_bias_b, d_skip, ssd_norm_w,
                             w_out, w_ple_in, w_ple_gate, final_norm_w)
    return (y_prompt, y_sample)
```

```python
import functools
import math

import jax
import jax.numpy as jnp
import numpy as np
from jax import lax
from jax.experimental import pallas as pl
from jax.experimental.pallas import tpu as pltpu

F32 = jnp.float32
BF16 = jnp.bfloat16

D_MODEL = 1024
D_FOURIER = 512
N_FGROUPS = 8
F_GROUP = 64
D_SSM = 1536
HEAD_DIM = 64
N_HEADS = 24
N_BC_GROUPS = 4
HEADS_PER_GROUP = 6
GROUP_W = HEADS_PER_GROUP * HEAD_DIM
D_STATE = 128
D_CONV = 5
CONV_DIM = 2560
CHUNK = 128
D_PLE = 256
D_IN_PROJ = 5168
D_IN_PAD = 5248
DT_COL = 5120
EPS = 1e-6

LANES = 128
FFT_N2 = 16
FFT_K1_TILE = 16
VMEM_LIMIT = 56 * 1024 * 1024


def _cparams(n_axes):
    return pltpu.CompilerParams(
        dimension_semantics=("arbitrary",) * n_axes, vmem_limit_bytes=VMEM_LIMIT
    )


def _silu(v):
    return v * jax.nn.sigmoid(v)


def _in_proj_kernel(x_ref, nw_ref, w_ref, zf_ref, uf_ref, zs_ref, xbc_ref, dt_ref):
    x = x_ref[...]
    ms = jnp.mean(x * x, axis=-1, keepdims=True)
    u = ((x * lax.rsqrt(ms + EPS)) * nw_ref[...]).astype(BF16)

    def seg(c0, width):
        return jnp.dot(u, w_ref[:, c0 : c0 + width], preferred_element_type=F32)

    zf_ref[...] = seg(0, 512).astype(BF16)
    uf_ref[...] = seg(512, 512).astype(BF16)
    for j in range(3):
        zs_ref[:, 512 * j : 512 * (j + 1)] = seg(1024 + 512 * j, 512).astype(BF16)
    for j in range(5):
        xbc_ref[:, 512 * j : 512 * (j + 1)] = seg(2560 + 512 * j, 512).astype(BF16)
    dt_ref[...] = seg(DT_COL, LANES)


def _in_proj(x2, norm_w, w_in_b, tm):
    t = x2.shape[0]
    row = lambda i: (i, 0)
    fixed = lambda i: (0, 0)
    return pl.pallas_call(
        _in_proj_kernel,
        grid=(t // tm,),
        in_specs=[
            pl.BlockSpec((tm, D_MODEL), row),
            pl.BlockSpec((1, D_MODEL), fixed),
            pl.BlockSpec((D_MODEL, D_IN_PAD), fixed),
        ],
        out_specs=[
            pl.BlockSpec((tm, 512), row),
            pl.BlockSpec((tm, 512), row),
            pl.BlockSpec((tm, D_SSM), row),
            pl.BlockSpec((tm, CONV_DIM), row),
            pl.BlockSpec((tm, LANES), row),
        ],
        out_shape=[
            jax.ShapeDtypeStruct((t, 512), BF16),
            jax.ShapeDtypeStruct((t, 512), BF16),
            jax.ShapeDtypeStruct((t, D_SSM), BF16),
            jax.ShapeDtypeStruct((t, CONV_DIM), BF16),
            jax.ShapeDtypeStruct((t, LANES), F32),
        ],
        compiler_params=_cparams(1),
        name="in_proj",
    )(x2, norm_w, w_in_b)


def _fft1_kernel(x3_ref, w1_ref, tc_ref, ts_ref, re_ref, mi_ref):
    n1 = x3_ref.shape[1]
    pq = jnp.dot(w1_ref[...], x3_ref[0], preferred_element_type=F32)
    p = pq[:n1]
    q = pq[n1:]
    c = jnp.concatenate([tc_ref[0]] * 4, axis=1)
    s = jnp.concatenate([ts_ref[0]] * 4, axis=1)
    re_ref[0, 0] = (p * c - q * s).astype(BF16)
    mi_ref[0, 0] = (p * s + q * c).astype(BF16)


def _fft3_kernel(re_ref, mi_ref, k3_ref, cs_ref, wbd_ref, zf_ref, y_ref, *, scale):
    rows = FFT_N2 * FFT_K1_TILE
    rhs = jnp.concatenate(
        [re_ref[0].reshape(rows, 512), mi_ref[0].reshape(rows, 512)], axis=0
    )
    z = jnp.dot(k3_ref[...], rhs, preferred_element_type=F32)
    zcat = jnp.concatenate([z[:rows], z[rows:]], axis=1).astype(BF16)
    four = jnp.dot(zcat, cs_ref[...], preferred_element_type=F32) * scale
    y = jnp.dot(four.astype(BF16), wbd_ref[...], preferred_element_type=F32)
    zf = zf_ref[0].reshape(rows, 512).astype(F32)
    y_ref[0] = (y * _silu(zf)).astype(BF16).reshape(FFT_N2, FFT_K1_TILE, 512)


def _dft_cos_sin(n):
    j = np.arange(n, dtype=np.int64)
    ang = 2.0 * np.pi * ((j[:, None] * j[None, :]) % n).astype(np.float64) / n
    return np.cos(ang), np.sin(ang)


@functools.lru_cache(maxsize=None)
def _fft_constants(s):
    n2 = FFT_N2
    n1 = s // n2
    c1, s1 = _dft_cos_sin(n1)
    w1 = np.concatenate([c1, s1], axis=0)
    k1 = np.arange(n1, dtype=np.int64)
    s2 = np.arange(n2, dtype=np.int64)
    ang = 2.0 * np.pi * ((s2[:, None] * k1[None, :]) % s).astype(np.float64) / s
    tc = np.repeat(np.cos(ang)[:, :, None], LANES, axis=2)
    ts = np.repeat(np.sin(ang)[:, :, None], LANES, axis=2)
    c2, s2m = _dft_cos_sin(n2)
    eye = np.eye(FFT_K1_TILE)
    kc = np.kron(c2, eye)
    ks = np.kron(s2m, eye)
    k3 = np.block([[kc, -ks], [ks, kc]])
    cc, sc = _dft_cos_sin(F_GROUP)
    eye_g = np.eye(N_FGROUPS)
    cs = np.concatenate([np.kron(eye_g, cc), -np.kron(eye_g, sc)], axis=0)
    return (
        jnp.asarray(w1, BF16),
        jnp.asarray(tc, F32),
        jnp.asarray(ts, F32),
        jnp.asarray(k3, BF16),
        jnp.asarray(cs, BF16),
    )


def _fnet(u_f, z_f, wbd_b):
    b, s, _ = u_f.shape
    n2 = FFT_N2
    n1 = s // n2
    w1, tc, ts, k3, cs = _fft_constants(s)
    x3 = u_f.reshape(b, n1, n2 * 512)
    a_shape = jax.ShapeDtypeStruct((b, n2, n1, 512), BF16)
    re, mi = pl.pallas_call(
        _fft1_kernel,
        grid=(b, n2),
        in_specs=[
            pl.BlockSpec((1, n1, 512), lambda i, j: (i, 0, j)),
            pl.BlockSpec((2 * n1, n1), lambda i, j: (0, 0)),
            pl.BlockSpec((1, n1, LANES), lambda i, j: (j, 0, 0)),
            pl.BlockSpec((1, n1, LANES), lambda i, j: (j, 0, 0)),
        ],
        out_specs=[
            pl.BlockSpec((1, 1, n1, 512), lambda i, j: (i, j, 0, 0)),
            pl.BlockSpec((1, 1, n1, 512), lambda i, j: (i, j, 0, 0)),
        ],
        out_shape=[a_shape, a_shape],
        compiler_params=_cparams(2),
        name="fft1",
    )(x3, w1, tc, ts)

    kt = FFT_K1_TILE
    blk = pl.BlockSpec((1, n2, kt, 512), lambda i, j: (i, 0, j, 0))
    fixed = lambda i, j: (0, 0)
    scale = 1.0 / math.sqrt(float(s) * F_GROUP)
    y4 = pl.pallas_call(
        functools.partial(_fft3_kernel, scale=scale),
        grid=(b, n1 // kt),
        in_specs=[
            blk,
            blk,
            pl.BlockSpec(k3.shape, fixed),
            pl.BlockSpec(cs.shape, fixed),
            pl.BlockSpec((512, 512), fixed),
            blk,
        ],
        out_specs=blk,
        out_shape=jax.ShapeDtypeStruct((b, n2, n1, 512), BF16),
        compiler_params=_cparams(2),
        name="fft3",
    )(re, mi, k3, cs, wbd_b, z_f.reshape(b, n2, n1, 512))
    return y4.reshape(b, s, 512)


def _split3_dot(tri, v):
    v1 = v.astype(BF16)
    r1 = v - v1.astype(F32)
    v2 = r1.astype(BF16)
    v3 = (r1 - v2.astype(F32)).astype(BF16)
    d = lambda w: jnp.dot(tri, w, preferred_element_type=F32)
    return d(v1) + d(v2) + d(v3)


def _ssd_chunk(xs, bm, cm, dt_raw, bias, alog, tri, expand, state_ref, *, lane0, forward):
    dt = jnp.maximum(dt_raw + bias, 0.0) + jnp.log1p(jnp.exp(-jnp.abs(dt_raw + bias)))
    a = dt * (-jnp.exp(alog))
    acs = _split3_dot(tri, a)
    total = acs[CHUNK - 1 : CHUNK, :] if forward else acs[0:1, :]
    w_st = dt * jnp.exp(total - acs)
    e_acs = jnp.exp(acs)
    e_tot = jnp.broadcast_to(jnp.exp(total), (8, LANES))

    ex = lambda v: jnp.dot(v.astype(BF16), expand, preferred_element_type=F32)
    x_in = (xs * ex(dt)).astype(BF16)
    x_st = (xs * ex(w_st)).astype(BF16)
    e_acs_x = ex(e_acs)
    e_tot_x = ex(e_tot)[0:1, :]

    acs_t = acs.T
    li = lax.broadcasted_iota(jnp.int32, (CHUNK, CHUNK), 0)
    si = lax.broadcasted_iota(jnp.int32, (CHUNK, CHUNK), 1)
    keep = (li >= si) if forward else (li <= si)
    lane = lax.broadcasted_iota(jnp.int32, (CHUNK, LANES), 1)
    left = lane < HEAD_DIM

    ys = []
    for g in range(N_BC_GROUPS):
        bg = bm[:, D_STATE * g : D_STATE * (g + 1)]
        cg = cm[:, D_STATE * g : D_STATE * (g + 1)].astype(BF16)
        cb = lax.dot_general(
            cg, bg.astype(BF16), (((1,), (1,)), ((), ())), preferred_element_type=F32
        )
        st = state_ref[g]
        gsl = slice(GROUP_W * g, GROUP_W * (g + 1))
        y_off = jnp.dot(cg, st.astype(BF16), preferred_element_type=F32) * e_acs_x[:, gsl]
        for j in range(HEADS_PER_GROUP // 2):
            pair = (HEADS_PER_GROUP // 2) * g + j
            gmats = []
            for h in (2 * pair, 2 * pair + 1):
                col = acs[:, lane0 + h : lane0 + h + 1]
                row = acs_t[lane0 + h : lane0 + h + 1, :]
                gmats.append((jnp.exp(jnp.where(keep, col - row, -jnp.inf)) * cb).astype(BF16))
            xp = x_in[:, LANES * pair : LANES * (pair + 1)]
            zero = jnp.zeros_like(xp)
            rhs = jnp.concatenate([jnp.where(left, xp, zero), jnp.where(left, zero, xp)], axis=0)
            y_pair = jnp.dot(jnp.concatenate(gmats, axis=1), rhs, preferred_element_type=F32)
            ys.append(y_pair + y_off[:, LANES * j : LANES * (j + 1)])
        upd = jnp.dot(bg.T.astype(BF16), x_st[:, gsl], preferred_element_type=F32)
        state_ref[g] = st * e_tot_x[:, gsl] + upd
    return jnp.concatenate(ys, axis=1)


def _ssd_fwd_kernel(
    xprev_ref, x_ref, xnext_ref, dt_ref, cw_ref, cbias_ref, bias_ref, alog_ref, tri_ref,
    expand_ref, yf_ref, act_ref, ext_scr, state_scr, *, tm,
):
    i = pl.program_id(1)
    nt = pl.num_programs(1)

    @pl.when(i == 0)
    def _():
        state_scr[...] = jnp.zeros_like(state_scr)

    prev = xprev_ref[0].astype(F32)[8:16]
    nxt = xnext_ref[0].astype(F32)[0:8]
    ext_scr[0:8, :] = jnp.where(i == 0, jnp.zeros_like(prev), prev)
    ext_scr[8 : 8 + tm, :] = x_ref[0].astype(F32)
    ext_scr[8 + tm : 16 + tm, :] = jnp.where(i == nt - 1, jnp.zeros_like(nxt), nxt)

    for r in range(tm // CHUNK):
        for c in range(CONV_DIM // 512):
            cols = slice(512 * c, 512 * (c + 1))
            acc = cbias_ref[:, cols] + jnp.zeros((CHUNK, 512), F32)
            for k in range(D_CONV):
                tap = ext_scr[pl.ds(8 - D_CONV // 2 + k + CHUNK * r, CHUNK), cols]
                acc = acc + tap * cw_ref[k : k + 1, cols]
            act_ref[0, CHUNK * r : CHUNK * (r + 1), cols] = _silu(acc).astype(BF16)

    def body(ci, carry):
        r0 = pl.multiple_of(ci * CHUNK, CHUNK)
        act = act_ref[0, pl.ds(r0, CHUNK), :].astype(F32)
        y = _ssd_chunk(
            act[:, :D_SSM], act[:, D_SSM : D_SSM + 512], act[:, D_SSM + 512 :],
            dt_ref[0, pl.ds(r0, CHUNK), :], bias_ref[...], alog_ref[...], tri_ref[...],
            expand_ref[...], state_scr, lane0=0, forward=True,
        )
        yf_ref[0, pl.ds(r0, CHUNK), :] = y
        return carry

    lax.fori_loop(0, tm // CHUNK, body, 0)


def _ssd_bwd_kernel(
    act_ref, dt_ref, yf_ref, zs_ref, bias_ref, alog_ref, tri_ref, expand_ref, dskip_ref,
    nw_ref, ys_ref, state_scr, *, tm,
):
    i = pl.program_id(1)

    @pl.when(i == 0)
    def _():
        state_scr[...] = jnp.zeros_like(state_scr)

    nc = tm // CHUNK

    def body(cj, carry):
        r0 = pl.multiple_of((nc - 1 - cj) * CHUNK, CHUNK)
        act = act_ref[0, pl.ds(r0, CHUNK), :].astype(F32)
        xs = act[:, :D_SSM]
        y = _ssd_chunk(
            xs, act[:, D_SSM : D_SSM + 512], act[:, D_SSM + 512 :],
            dt_ref[0, pl.ds(r0, CHUNK), :], bias_ref[...], alog_ref[...], tri_ref[...],
            expand_ref[...], state_scr, lane0=N_HEADS, forward=False,
        )
        y = y + yf_ref[0, pl.ds(r0, CHUNK), :] + dskip_ref[...] * xs
        y = y * _silu(zs_ref[0, pl.ds(r0, CHUNK), :].astype(F32))
        outs = []
        for g in range(N_BC_GROUPS):
            yg = y[:, GROUP_W * g : GROUP_W * (g + 1)]
            ms = jnp.mean(yg * yg, axis=-1, keepdims=True)
            outs.append(yg * lax.rsqrt(ms + EPS))
        yn = jnp.concatenate(outs, axis=1) * nw_ref[...]
        ys_ref[0, pl.ds(r0, CHUNK), :] = yn.astype(BF16)
        return carry

    lax.fori_loop(0, nc, body, 0)


@functools.lru_cache(maxsize=None)
def _ssd_constants():
    tri_f = np.tril(np.ones((CHUNK, CHUNK)))
    tri_b = np.triu(np.ones((CHUNK, CHUNK)))
    ef = np.zeros((LANES, D_SSM))
    eb = np.zeros((LANES, D_SSM))
    for h in range(N_HEADS):
        ef[h, HEAD_DIM * h : HEAD_DIM * (h + 1)] = 1.0
        eb[N_HEADS + h, HEAD_DIM * h : HEAD_DIM * (h + 1)] = 1.0
    return tuple(jnp.asarray(v, BF16) for v in (tri_f, tri_b, ef, eb))


def _ssd(xbc, dt_raw, z_s, conv_w, conv_b, bias_p, alog_p, dskip_x, ssd_nw, tm):
    b, s, _ = xbc.shape
    nt = s // tm
    halo = 16
    tri_f, tri_b, ef, eb = _ssd_constants()
    fixed = lambda bi, i: (0, 0)
    tile = lambda w: pl.BlockSpec((1, tm, w), lambda bi, i: (bi, i, 0))
    state = pltpu.VMEM((N_BC_GROUPS, D_STATE, GROUP_W), F32)

    y_fwd, act = pl.pallas_call(
        functools.partial(_ssd_fwd_kernel, tm=tm),
        grid=(b, nt),
        in_specs=[
            pl.BlockSpec(
                (1, halo, CONV_DIM),
                lambda bi, i: (bi, jnp.maximum(i * (tm // halo) - 1, 0), 0),
            ),
            tile(CONV_DIM),
            pl.BlockSpec(
                (1, halo, CONV_DIM),
                lambda bi, i: (bi, jnp.minimum((i + 1) * (tm // halo), s // halo - 1), 0),
            ),
            tile(LANES),
            pl.BlockSpec((D_CONV, CONV_DIM), fixed),
            pl.BlockSpec((1, CONV_DIM), fixed),
            pl.BlockSpec((1, LANES), fixed),
            pl.BlockSpec((1, LANES), fixed),
            pl.BlockSpec((CHUNK, CHUNK), fixed),
            pl.BlockSpec((LANES, D_SSM), fixed),
        ],
        out_specs=[tile(D_SSM), tile(CONV_DIM)],
        out_shape=[
            jax.ShapeDtypeStruct((b, s, D_SSM), F32),
            jax.ShapeDtypeStruct((b, s, CONV_DIM), BF16),
        ],
        scratch_shapes=[pltpu.VMEM((tm + 16, CONV_DIM), F32), state],
        compiler_params=_cparams(2),
        name="ssd_fwd",
    )(xbc, xbc, xbc, dt_raw, conv_w, conv_b, bias_p, alog_p, tri_f, ef)

    rtile = lambda w: pl.BlockSpec((1, tm, w), lambda bi, i: (bi, nt - 1 - i, 0))
    return pl.pallas_call(
        functools.partial(_ssd_bwd_kernel, tm=tm),
        grid=(b, nt),
        in_specs=[
            rtile(CONV_DIM),
            rtile(LANES),
            rtile(D_SSM),
            rtile(D_SSM),
            pl.BlockSpec((1, LANES), fixed),
            pl.BlockSpec((1, LANES), fixed),
            pl.BlockSpec((CHUNK, CHUNK), fixed),
            pl.BlockSpec((LANES, D_SSM), fixed),
            pl.BlockSpec((1, D_SSM), fixed),
            pl.BlockSpec((1, D_SSM), fixed),
        ],
        out_specs=rtile(D_SSM),
        out_shape=jax.ShapeDtypeStruct((b, s, D_SSM), BF16),
        scratch_shapes=[state],
        compiler_params=_cparams(2),
        name="ssd_bwd",
    )(act, dt_raw, y_fwd, z_s, bias_p, alog_p, tri_b, eb, dskip_x, ssd_nw)


def _epilogue_kernel(x_ref, yf_ref, ys_ref, p_ref, wo_ref, wg_ref, wp_ref, fw_ref, o_ref):
    mix = jnp.concatenate([yf_ref[...], ys_ref[...]], axis=1)
    h = x_ref[...] + jnp.dot(mix, wo_ref[...], preferred_element_type=F32)
    gate = jax.nn.sigmoid(jnp.dot(h.astype(BF16), wg_ref[...], preferred_element_type=F32))
    ple = jnp.dot(p_ref[...].astype(BF16), wp_ref[...], preferred_element_type=F32)
    h = h + ple * gate
    ms = jnp.mean(h * h, axis=-1, keepdims=True)
    o_ref[...] = (h * lax.rsqrt(ms + EPS)) * fw_ref[...]


def _epilogue(x2, y_f, y_s, p2, w_out_b, w_gate_b, w_ple_b, final_w, tm):
    t = x2.shape[0]
    row = lambda i: (i, 0)
    fixed = lambda i: (0, 0)
    return pl.pallas_call(
        _epilogue_kernel,
        grid=(t // tm,),
        in_specs=[
            pl.BlockSpec((tm, D_MODEL), row),
            pl.BlockSpec((tm, 512), row),
            pl.BlockSpec((tm, D_SSM), row),
            pl.BlockSpec((tm, D_PLE), row),
            pl.BlockSpec((2048, D_MODEL), fixed),
            pl.BlockSpec((D_MODEL, D_MODEL), fixed),
            pl.BlockSpec((D_PLE, D_MODEL), fixed),
            pl.BlockSpec((1, D_MODEL), fixed),
        ],
        out_specs=pl.BlockSpec((tm, D_MODEL), row),
        out_shape=jax.ShapeDtypeStruct((t, D_MODEL), F32),
        compiler_params=_cparams(1),
        name="epilogue",
    )(x2, y_f, y_s, p2, w_out_b, w_gate_b, w_ple_b, final_w)


def _trunk(x, p, norm_w, w_in, w_fmix, conv_w, conv_b, a_log_f, a_log_b, dt_bias_f,
           dt_bias_b, d_skip, ssd_norm_w, w_out, w_ple_in, w_ple_gate, final_norm_w):
    b, s, _ = x.shape
    t = b * s
    tm = 512
    x2 = x.reshape(t, D_MODEL)

    w_in_b = jnp.pad(w_in[0], ((0, 0), (0, D_IN_PAD - D_IN_PROJ))).astype(BF16)
    z_f, u_f, z_s, xbc, dt_raw = _in_proj(x2, norm_w[0][None, :], w_in_b, tm)

    wbd = jax.scipy.linalg.block_diag(*[w_fmix[0, g] for g in range(N_FGROUPS)]).astype(BF16)
    y_f = _fnet(u_f.reshape(b, s, 512), z_f.reshape(b, s, 512), wbd)

    pad = jnp.zeros((LANES - 2 * N_HEADS,), F32)
    bias_p = jnp.concatenate([dt_bias_f[0], dt_bias_b[0], pad])[None, :]
    alog_p = jnp.concatenate([a_log_f[0], a_log_b[0], pad])[None, :]
    dskip_x = jnp.repeat(d_skip[0], HEAD_DIM)[None, :]
    y_s = _ssd(
        xbc.reshape(b, s, CONV_DIM), dt_raw.reshape(b, s, LANES), z_s.reshape(b, s, D_SSM),
        conv_w[0], conv_b[0][None, :], bias_p, alog_p, dskip_x, ssd_norm_w[0][None, :], tm,
    )

    out = _epilogue(
        x2, y_f.reshape(t, 512), y_s.reshape(t, D_SSM), p[0].reshape(t, D_PLE),
        w_out[0].astype(BF16), w_ple_gate[0].astype(BF16), w_ple_in[0].astype(BF16),
        final_norm_w[None, :], tm,
    )
    return out.reshape(b, s, D_MODEL)


def kernel(x_prompt, x_sample, p_prompt, p_sample, norm_w, w_in, w_fmix, conv_w, conv_b, a_log_f, a_log_b, dt_bias_f, dt_bias_b, d_skip, ssd_norm_w, w_out, w_ple_in, w_ple_gate, final_norm_w):
    weights = (norm_w, w_in, w_fmix, conv_w, conv_b, a_log_f, a_log_b, dt_bias_f, dt_bias_b,
               d_skip, ssd_norm_w, w_out, w_ple_in, w_ple_gate, final_norm_w)
    return (_trunk(x_prompt, p_prompt, *weights), _trunk(x_sample, p_sample, *weights))
```

```python
import functools
import math

import jax
import jax.numpy as jnp
import numpy as np
from jax import lax
from jax.experimental import pallas as pl
from jax.experimental.pallas import tpu as pltpu

F32 = jnp.float32
BF16 = jnp.bfloat16

D_MODEL = 1024
D_FOURIER = 512
N_FGROUPS = 8
F_GROUP = 64
D_SSM = 1536
HEAD_DIM = 64
N_HEADS = 24
N_BC_GROUPS = 4
HEADS_PER_GROUP = 6
GROUP_W = HEADS_PER_GROUP * HEAD_DIM
D_STATE = 128
D_CONV = 5
CONV_DIM = 2560
CHUNK = 128
D_PLE = 256
D_IN_PROJ = 5168
D_IN_PAD = 5248
DT_COL = 5120
EPS = 1e-6

LANES = 128
HALO = 8
FFT_N2 = 16
FFT_K1_TILE = 16
VMEM_LIMIT = 56 * 1024 * 1024


def _cparams(n_axes):
    return pltpu.CompilerParams(
        dimension_semantics=("arbitrary",) * n_axes, vmem_limit_bytes=VMEM_LIMIT
    )


def _sigmoid(v):
    return 0.5 + 0.5 * jnp.tanh(0.5 * v)


def _silu(v):
    hv = 0.5 * v
    return hv + hv * jnp.tanh(hv)


def _in_proj_kernel(xp_ref, x_ref, xn_ref, nw_ref, w_ref, cw_ref, cb_ref, dtb_ref,
                    zf_ref, uf_ref, zs_ref, act_ref, dt_ref, *, tm):
    i = pl.program_id(1)
    nt = pl.num_programs(1)

    def norm(x):
        ms = jnp.mean(x * x, axis=-1, keepdims=True)
        return (x * lax.rsqrt(ms + EPS)) * nw_ref[...]

    u32 = norm(x_ref[0])
    up = jnp.where(i == 0, 0.0, norm(xp_ref[0]))
    un = jnp.where(i == nt - 1, 0.0, norm(xn_ref[0]))
    u = u32.astype(BF16)
    u_ext = jnp.concatenate([up, u32, un], axis=0).astype(BF16)

    def seg(lhs, c0, width):
        return jnp.dot(lhs, w_ref[:, c0 : c0 + width], preferred_element_type=F32)

    zf_ref[0] = seg(u, 0, 512).astype(BF16)
    uf_ref[0] = seg(u, 512, 512).astype(BF16)
    for j in range(3):
        zs_ref[0, :, 512 * j : 512 * (j + 1)] = seg(u, 1024 + 512 * j, 512).astype(BF16)
    dt_pre = seg(u, DT_COL, LANES) + dtb_ref[...]
    dt_ref[0] = jnp.maximum(dt_pre, 0.0) + jnp.log1p(jnp.exp(-jnp.abs(dt_pre)))
    mid = D_CONV // 2
    for j in range(CONV_DIM // 512):
        cols = slice(512 * j, 512 * (j + 1))
        xe = seg(u_ext, 2560 + 512 * j, 512)
        acc = cb_ref[:, cols] + xe[HALO : HALO + tm] * cw_ref[mid : mid + 1, cols]
        for k in range(D_CONV):
            if k != mid:
                r0 = HALO - mid + k
                acc = acc + xe[r0 : r0 + tm] * cw_ref[k : k + 1, cols]
        act_ref[0, :, cols] = _silu(acc).astype(BF16)


def _in_proj(x, norm_w, w_in_b, conv_w, conv_b, dt_bias_p, tm):
    b, s, _ = x.shape
    tile = lambda w: pl.BlockSpec((1, tm, w), lambda bi, i: (bi, i, 0))
    fixed = lambda bi, i: (0, 0)
    outs = [(512, BF16), (512, BF16), (D_SSM, BF16), (CONV_DIM, BF16), (LANES, F32)]
    return pl.pallas_call(
        functools.partial(_in_proj_kernel, tm=tm),
        grid=(b, s // tm),
        in_specs=[
            pl.BlockSpec(
                (1, HALO, D_MODEL),
                lambda bi, i: (bi, jnp.maximum(i * (tm // HALO) - 1, 0), 0),
            ),
            tile(D_MODEL),
            pl.BlockSpec(
                (1, HALO, D_MODEL),
                lambda bi, i: (bi, jnp.minimum((i + 1) * (tm // HALO), s // HALO - 1), 0),
            ),
            pl.BlockSpec((1, D_MODEL), fixed),
            pl.BlockSpec((D_MODEL, D_IN_PAD), fixed),
            pl.BlockSpec((D_CONV, CONV_DIM), fixed),
            pl.BlockSpec((1, CONV_DIM), fixed),
            pl.BlockSpec((1, LANES), fixed),
        ],
        out_specs=[tile(w) for w, _ in outs],
        out_shape=[jax.ShapeDtypeStruct((b, s, w), d) for w, d in outs],
        compiler_params=_cparams(2),
        name="in_proj",
    )(x, x, x, norm_w, w_in_b, conv_w, conv_b, dt_bias_p)


def _fft1_kernel(x3_ref, w1_ref, tc_ref, ts_ref, re_ref, mi_ref):
    n1 = x3_ref.shape[1]
    pq = jnp.dot(w1_ref[...], x3_ref[0], preferred_element_type=F32)
    p = pq[:n1]
    q = pq[n1:]
    c = jnp.concatenate([tc_ref[0]] * 4, axis=1)
    s = jnp.concatenate([ts_ref[0]] * 4, axis=1)
    re_ref[0, 0] = (p * c - q * s).astype(BF16)
    mi_ref[0, 0] = (p * s + q * c).astype(BF16)


def _fft3_kernel(re_ref, mi_ref, k3_ref, cs_ref, wbd_ref, zf_ref, y_ref, *, scale):
    rows = FFT_N2 * FFT_K1_TILE
    rhs = jnp.concatenate(
        [re_ref[0].reshape(rows, 512), mi_ref[0].reshape(rows, 512)], axis=0
    )
    z = jnp.dot(k3_ref[...], rhs, preferred_element_type=F32)
    zcat = jnp.concatenate([z[:rows], z[rows:]], axis=1).astype(BF16)
    four = jnp.dot(zcat, cs_ref[...], preferred_element_type=F32) * scale
    y = jnp.dot(four.astype(BF16), wbd_ref[...], preferred_element_type=F32)
    zf = zf_ref[0].reshape(rows, 512).astype(F32)
    y_ref[0] = (y * _silu(zf)).astype(BF16).reshape(FFT_N2, FFT_K1_TILE, 512)


def _dft_cos_sin(n):
    j = np.arange(n, dtype=np.int64)
    ang = 2.0 * np.pi * ((j[:, None] * j[None, :]) % n).astype(np.float64) / n
    return np.cos(ang), np.sin(ang)


@functools.lru_cache(maxsize=None)
def _fft_constants(s):
    n2 = FFT_N2
    n1 = s // n2
    c1, s1 = _dft_cos_sin(n1)
    w1 = np.concatenate([c1, s1], axis=0)
    k1 = np.arange(n1, dtype=np.int64)
    s2 = np.arange(n2, dtype=np.int64)
    ang = 2.0 * np.pi * ((s2[:, None] * k1[None, :]) % s).astype(np.float64) / s
    tc = np.repeat(np.cos(ang)[:, :, None], LANES, axis=2)
    ts = np.repeat(np.sin(ang)[:, :, None], LANES, axis=2)
    c2, s2m = _dft_cos_sin(n2)
    eye = np.eye(FFT_K1_TILE)
    kc = np.kron(c2, eye)
    ks = np.kron(s2m, eye)
    k3 = np.block([[kc, -ks], [ks, kc]])
    cc, sc = _dft_cos_sin(F_GROUP)
    eye_g = np.eye(N_FGROUPS)
    cs = np.concatenate([np.kron(eye_g, cc), -np.kron(eye_g, sc)], axis=0)
    return (
        jnp.asarray(w1, BF16),
        jnp.asarray(tc, F32),
        jnp.asarray(ts, F32),
        jnp.asarray(k3, BF16),
        jnp.asarray(cs, BF16),
    )


def _fnet(u_f, z_f, wbd_b):
    b, s, _ = u_f.shape
    n2 = FFT_N2
    n1 = s // n2
    w1, tc, ts, k3, cs = _fft_constants(s)
    x3 = u_f.reshape(b, n1, n2 * 512)
    a_shape = jax.ShapeDtypeStruct((b, n2, n1, 512), BF16)
    re, mi = pl.pallas_call(
        _fft1_kernel,
        grid=(b, n2),
        in_specs=[
            pl.BlockSpec((1, n1, 512), lambda i, j: (i, 0, j)),
            pl.BlockSpec((2 * n1, n1), lambda i, j: (0, 0)),
            pl.BlockSpec((1, n1, LANES), lambda i, j: (j, 0, 0)),
            pl.BlockSpec((1, n1, LANES), lambda i, j: (j, 0, 0)),
        ],
        out_specs=[
            pl.BlockSpec((1, 1, n1, 512), lambda i, j: (i, j, 0, 0)),
            pl.BlockSpec((1, 1, n1, 512), lambda i, j: (i, j, 0, 0)),
        ],
        out_shape=[a_shape, a_shape],
        compiler_params=_cparams(2),
        name="fft1",
    )(x3, w1, tc, ts)

    kt = FFT_K1_TILE
    blk = pl.BlockSpec((1, n2, kt, 512), lambda i, j: (i, 0, j, 0))
    fixed = lambda i, j: (0, 0)
    scale = 1.0 / math.sqrt(float(s) * F_GROUP)
    y4 = pl.pallas_call(
        functools.partial(_fft3_kernel, scale=scale),
        grid=(b, n1 // kt),
        in_specs=[
            blk,
            blk,
            pl.BlockSpec(k3.shape, fixed),
            pl.BlockSpec(cs.shape, fixed),
            pl.BlockSpec((512, 512), fixed),
            blk,
        ],
        out_specs=blk,
        out_shape=jax.ShapeDtypeStruct((b, n2, n1, 512), BF16),
        compiler_params=_cparams(2),
        name="fft3",
    )(re, mi, k3, cs, wbd_b, z_f.reshape(b, n2, n1, 512))
    return y4.reshape(b, s, 512)


def _split3_dot(tri, v):
    v1 = v.astype(BF16)
    r1 = v - v1.astype(F32)
    v2 = r1.astype(BF16)
    v3 = (r1 - v2.astype(F32)).astype(BF16)
    d = lambda w: jnp.dot(tri, w, preferred_element_type=F32)
    return d(v1) + d(v2) + d(v3)


N_PANELS = 4
LOG2E = math.log2(math.e)


def _ssd_panels(dt, alog, tri, pan_ref, etot_ref, c, *, forward):
    a = dt * (-jnp.exp(alog))
    acs = _split3_dot(tri, a)
    total = acs[CHUNK - 1 : CHUNK, :] if forward else acs[0:1, :]
    acs2 = acs * LOG2E
    l2dt = jnp.log2(dt)
    pan_ref[c, 0] = acs2
    pan_ref[c, 1] = (acs2 - l2dt).T
    pan_ref[c, 2] = l2dt.T
    pan_ref[c, 3] = (dt * jnp.exp(total - acs)).T
    etot_ref[c] = jnp.broadcast_to(jnp.exp(total), (8, LANES))


def _ssd_chunk(act_ref, r0, pan_ref, etot_ref, ci, state_ref, gb_ref, emit, *, lane0, forward):
    rows = pl.ds(r0, CHUNK)
    e_tot = etot_ref[ci][0:1, :]

    li = lax.broadcasted_iota(jnp.int32, (CHUNK, CHUNK), 0)
    si = lax.broadcasted_iota(jnp.int32, (CHUNK, CHUNK), 1)
    keep = (li >= si) if forward else (li <= si)
    left = lax.broadcasted_iota(jnp.int32, (CHUNK, LANES), 1) < HEAD_DIM
    left_row = lax.broadcasted_iota(jnp.int32, (1, LANES), 1) < HEAD_DIM
    nt_dims = (((1,), (1,)), ((), ()))

    for g in range(N_BC_GROUPS):
        bg = act_ref[0, rows, pl.ds(D_SSM + D_STATE * g, D_STATE)]
        cg = act_ref[0, rows, pl.ds(D_SSM + 512 + D_STATE * g, D_STATE)]
        cb = lax.dot_general(cg, bg, nt_dims, preferred_element_type=F32)
        gb_ref[0] = jnp.where(keep, cb, 0.0)
        gb_ref[1] = bg.astype(F32).T
        y_off = jnp.dot(cg, state_ref[g].astype(BF16), preferred_element_type=F32)
        ys, sts = [], []
        for j in range(HEADS_PER_GROUP // 2):
            pair = (HEADS_PER_GROUP // 2) * g + j
            psl = slice(LANES * j, LANES * (j + 1))
            xp = act_ref[0, rows, pl.ds(LANES * pair, LANES)]
            zero = jnp.zeros_like(xp)
            xbd = jnp.concatenate([jnp.where(left, xp, zero), jnp.where(left, zero, xp)], axis=0)
            gl, wl, es, et = [], [], [], []
            for h in (2 * pair, 2 * pair + 1):
                hl = lane0 + h
                row = lambda p: pan_ref[ci, p, pl.ds(hl, 1), :]
                colb = jnp.broadcast_to(pan_ref[ci, 0][:, hl : hl + 1], (CHUNK, LANES))
                dec = jnp.exp2(jnp.minimum(colb - row(1), row(2)))
                gl.append((dec * gb_ref[0]).astype(BF16))
                wl.append((gb_ref[1] * row(3)).astype(BF16))
                es.append(jnp.exp2(colb))
                et.append(jnp.broadcast_to(e_tot[:, hl : hl + 1], (1, LANES)))
            lhs = jnp.concatenate(
                [jnp.concatenate(gl, axis=1), jnp.concatenate(wl, axis=1)], axis=0
            )
            res = jnp.dot(lhs, xbd, preferred_element_type=F32)
            ys.append(res[:CHUNK] + y_off[:, psl] * jnp.where(left, es[0], es[1]))
            sts.append(
                state_ref[g, :, psl] * jnp.where(left_row, et[0], et[1]) + res[CHUNK:]
            )
        state_ref[g] = jnp.concatenate(sts, axis=1)
        emit(g, jnp.concatenate(ys, axis=1))


def _ssd_fwd_kernel(act_ref, dt_ref, alog_ref, tri_ref, yf_ref, state_scr, pan_scr, etot_scr,
                    gb_scr, *, tm):
    @pl.when(pl.program_id(1) == 0)
    def _():
        state_scr[...] = jnp.zeros_like(state_scr)

    for c in range(tm // CHUNK):
        _ssd_panels(dt_ref[0, CHUNK * c : CHUNK * (c + 1), :], alog_ref[...], tri_ref[...],
                    pan_scr, etot_scr, c, forward=True)

    def body(ci, carry):
        r0 = pl.multiple_of(ci * CHUNK, CHUNK)

        def emit(g, y):
            yf_ref[0, pl.ds(r0, CHUNK), pl.ds(GROUP_W * g, GROUP_W)] = y

        _ssd_chunk(act_ref, r0, pan_scr, etot_scr, ci, state_scr, gb_scr, emit, lane0=0,
                   forward=True)
        return carry

    lax.fori_loop(0, tm // CHUNK, body, 0)


def _ssd_bwd_kernel(act_ref, dt_ref, yf_ref, zs_ref, alog_ref, tri_ref, dskip_ref,
                    nw_ref, ys_ref, state_scr, pan_scr, etot_scr, gb_scr, *, tm):
    @pl.when(pl.program_id(1) == 0)
    def _():
        state_scr[...] = jnp.zeros_like(state_scr)

    nc = tm // CHUNK
    for c in range(nc):
        _ssd_panels(dt_ref[0, CHUNK * c : CHUNK * (c + 1), :], alog_ref[...], tri_ref[...],
                    pan_scr, etot_scr, c, forward=False)

    def body(cj, carry):
        ci = nc - 1 - cj
        r0 = pl.multiple_of(ci * CHUNK, CHUNK)
        rows = pl.ds(r0, CHUNK)

        def emit(g, y):
            gsl = pl.ds(GROUP_W * g, GROUP_W)
            xs = act_ref[0, rows, gsl].astype(F32)
            y = y + yf_ref[0, rows, gsl] + dskip_ref[:, gsl] * xs
            y = y * _silu(zs_ref[0, rows, gsl].astype(F32))
            ms = jnp.mean(y * y, axis=-1, keepdims=True)
            ys_ref[0, rows, gsl] = ((y * lax.rsqrt(ms + EPS)) * nw_ref[:, gsl]).astype(BF16)

        _ssd_chunk(act_ref, r0, pan_scr, etot_scr, ci, state_scr, gb_scr, emit, lane0=N_HEADS,
                   forward=False)
        return carry

    lax.fori_loop(0, nc, body, 0)


@functools.lru_cache(maxsize=None)
def _ssd_constants():
    tri_f = np.tril(np.ones((CHUNK, CHUNK)))
    tri_b = np.triu(np.ones((CHUNK, CHUNK)))
    return jnp.asarray(tri_f, BF16), jnp.asarray(tri_b, BF16)


def _ssd(act, dt, z_s, alog_p, dskip_x, ssd_nw, tm):
    b, s, _ = act.shape
    nt = s // tm
    tri_f, tri_b = _ssd_constants()
    fixed = lambda bi, i: (0, 0)
    tile = lambda w: pl.BlockSpec((1, tm, w), lambda bi, i: (bi, i, 0))
    scratch = [
        pltpu.VMEM((N_BC_GROUPS, D_STATE, GROUP_W), F32),
        pltpu.VMEM((tm // CHUNK, N_PANELS, CHUNK, LANES), F32),
        pltpu.VMEM((tm // CHUNK, 8, LANES), F32),
        pltpu.VMEM((2, CHUNK, CHUNK), F32),
    ]

    y_fwd = pl.pallas_call(
        functools.partial(_ssd_fwd_kernel, tm=tm),
        grid=(b, nt),
        in_specs=[
            tile(CONV_DIM),
            tile(LANES),
            pl.BlockSpec((1, LANES), fixed),
            pl.BlockSpec((CHUNK, CHUNK), fixed),
        ],
        out_specs=tile(D_SSM),
        out_shape=jax.ShapeDtypeStruct((b, s, D_SSM), F32),
        scratch_shapes=scratch,
        compiler_params=_cparams(2),
        name="ssd_fwd",
    )(act, dt, alog_p, tri_f)

    rtile = lambda w: pl.BlockSpec((1, tm, w), lambda bi, i: (bi, nt - 1 - i, 0))
    return pl.pallas_call(
        functools.partial(_ssd_bwd_kernel, tm=tm),
        grid=(b, nt),
        in_specs=[
            rtile(CONV_DIM),
            rtile(LANES),
            rtile(D_SSM),
            rtile(D_SSM),
            pl.BlockSpec((1, LANES), fixed),
            pl.BlockSpec((CHUNK, CHUNK), fixed),
            pl.BlockSpec((1, D_SSM), fixed),
            pl.BlockSpec((1, D_SSM), fixed),
        ],
        out_specs=rtile(D_SSM),
        out_shape=jax.ShapeDtypeStruct((b, s, D_SSM), BF16),
        scratch_shapes=scratch,
        compiler_params=_cparams(2),
        name="ssd_bwd",
    )(act, dt, y_fwd, z_s, alog_p, tri_b, dskip_x, ssd_nw)


def _epilogue_kernel(x_ref, yf_ref, ys_ref, p_ref, wo_ref, wg_ref, wp_ref, fw_ref, o_ref):
    mix = jnp.concatenate([yf_ref[...], ys_ref[...]], axis=1)
    h = x_ref[...] + jnp.dot(mix, wo_ref[...], preferred_element_type=F32)
    gate = _sigmoid(jnp.dot(h.astype(BF16), wg_ref[...], preferred_element_type=F32))
    ple = jnp.dot(p_ref[...].astype(BF16), wp_ref[...], preferred_element_type=F32)
    h = h + ple * gate
    ms = jnp.mean(h * h, axis=-1, keepdims=True)
    o_ref[...] = (h * lax.rsqrt(ms + EPS)) * fw_ref[...]


def _epilogue(x2, y_f, y_s, p2, w_out_b, w_gate_b, w_ple_b, final_w, tm):
    t = x2.shape[0]
    row = lambda i: (i, 0)
    fixed = lambda i: (0, 0)
    return pl.pallas_call(
        _epilogue_kernel,
        grid=(t // tm,),
        in_specs=[
            pl.BlockSpec((tm, D_MODEL), row),
            pl.BlockSpec((tm, 512), row),
            pl.BlockSpec((tm, D_SSM), row),
            pl.BlockSpec((tm, D_PLE), row),
            pl.BlockSpec((2048, D_MODEL), fixed),
            pl.BlockSpec((D_MODEL, D_MODEL), fixed),
            pl.BlockSpec((D_PLE, D_MODEL), fixed),
            pl.BlockSpec((1, D_MODEL), fixed),
        ],
        out_specs=pl.BlockSpec((tm, D_MODEL), row),
        out_shape=jax.ShapeDtypeStruct((t, D_MODEL), F32),
        compiler_params=_cparams(1),
        name="epilogue",
    )(x2, y_f, y_s, p2, w_out_b, w_gate_b, w_ple_b, final_w)


def _trunk(x, p, norm_w, w_in, w_fmix, conv_w, conv_b, a_log_f, a_log_b, dt_bias_f,
           dt_bias_b, d_skip, ssd_norm_w, w_out, w_ple_in, w_ple_gate, final_norm_w):
    b, s, _ = x.shape
    t = b * s
    tm = 512

    pad = jnp.zeros((LANES - 2 * N_HEADS,), F32)
    bias_p = jnp.concatenate([dt_bias_f[0], dt_bias_b[0], pad])[None, :]
    alog_p = jnp.concatenate([a_log_f[0], a_log_b[0], pad])[None, :]
    dskip_x = jnp.repeat(d_skip[0], HEAD_DIM)[None, :]

    w_in_b = jnp.pad(w_in[0], ((0, 0), (0, D_IN_PAD - D_IN_PROJ))).astype(BF16)
    z_f, u_f, z_s, act, dt = _in_proj(
        x, norm_w[0][None, :], w_in_b, conv_w[0], conv_b[0][None, :], bias_p, tm
    )

    wbd = jax.scipy.linalg.block_diag(*[w_fmix[0, g] for g in range(N_FGROUPS)]).astype(BF16)
    y_f = _fnet(u_f, z_f, wbd)

    y_s = _ssd(act, dt, z_s, alog_p, dskip_x, ssd_norm_w[0][None, :], tm)

    out = _epilogue(
        x.reshape(t, D_MODEL), y_f.reshape(t, 512), y_s.reshape(t, D_SSM), p[0].reshape(t, D_PLE),
        w_out[0].astype(BF16), w_ple_gate[0].astype(BF16), w_ple_in[0].astype(BF16),
        final_norm_w[None, :], tm,
    )
    return out.reshape(b, s, D_MODEL)


def kernel(x_prompt, x_sample, p_prompt, p_sample, norm_w, w_in, w_fmix, conv_w, conv_b, a_log_f, a_log_b, dt_bias_f, dt_bias_b, d_skip, ssd_norm_w, w_out, w_ple_in, w_ple_gate, final_norm_w):
    weights = (norm_w, w_in, w_fmix, conv_w, conv_b, a_log_f, a_log_b, dt_bias_f, dt_bias_b,
               d_skip, ssd_norm_w, w_out, w_ple_in, w_ple_gate, final_norm_w)
    return (_trunk(x_prompt, p_prompt, *weights), _trunk(x_sample, p_sample, *weights))
```

```python
import functools
import math

import jax
import jax.numpy as jnp
import numpy as np
from jax import lax
from jax.experimental import pallas as pl
from jax.experimental.pallas import tpu as pltpu

F32 = jnp.float32
BF16 = jnp.bfloat16

D_MODEL = 1024
D_FOURIER = 512
N_FGROUPS = 8
F_GROUP = 64
D_SSM = 1536
HEAD_DIM = 64
N_HEADS = 24
N_BC_GROUPS = 4
HEADS_PER_GROUP = 6
GROUP_W = HEADS_PER_GROUP * HEAD_DIM
D_STATE = 128
D_CONV = 5
CONV_DIM = 2560
CHUNK = 128
D_PLE = 256
D_IN_PROJ = 5168
D_IN_PAD = 5248
DT_COL = 5120
EPS = 1e-6

LANES = 128
HALO = 8
SSD_FWD_UNROLL = 4
SSD_BWD_UNROLL = 2
FFT_N2 = 16
FFT_K1_TILE = 16
FFT_K1_STEP = 64
FFT_S2_STEP = 4
VMEM_LIMIT = 56 * 1024 * 1024


def _cparams(n_axes):
    return pltpu.CompilerParams(
        dimension_semantics=("arbitrary",) * n_axes, vmem_limit_bytes=VMEM_LIMIT
    )


def _sigmoid(v):
    return 0.5 + 0.5 * jnp.tanh(0.5 * v)


def _silu(v):
    hv = 0.5 * v
    return hv + hv * jnp.tanh(hv)


def _in_proj_kernel(xp_ref, x_ref, xn_ref, nw_ref, w_ref, cw_ref, cb_ref, dtb_ref,
                    zf_ref, uf_ref, zs_ref, act_ref, dt_ref, xe_scr, co_scr, *, tm):
    i = pl.program_id(1)
    nt = pl.num_programs(1)

    def norm(x):
        ms = jnp.mean(x * x, axis=-1, keepdims=True)
        return (x * lax.rsqrt(ms + EPS)) * nw_ref[...]

    u32 = norm(x_ref[0])
    up = jnp.where(i == 0, 0.0, norm(xp_ref[0]))
    un = jnp.where(i == nt - 1, 0.0, norm(xn_ref[0]))
    u = u32.astype(BF16)
    u_ext = jnp.concatenate([up, u32, un], axis=0).astype(BF16)

    def seg(lhs, c0, width):
        return jnp.dot(lhs, w_ref[:, c0 : c0 + width], preferred_element_type=F32)

    mid = D_CONV // 2

    def conv_block(j):
        xe = seg(u_ext, 2560 + 512 * j, 512)
        for c in range(512 // LANES):
            xe_scr[c] = xe[:, LANES * c : LANES * (c + 1)]
        for c in range(512 // LANES):
            cols = slice(512 * j + LANES * c, 512 * j + LANES * (c + 1))
            for par in range(2):
                acc = cb_ref[:, cols]
                for k in range(D_CONV):
                    tap = xe_scr[c, pl.ds(HALO - mid + k + par, tm // 2, stride=2), :]
                    acc = acc + tap * cw_ref[k : k + 1, cols]
                co_scr[c, pl.ds(par, tm // 2, stride=2), :] = _silu(acc)
        for c in range(512 // LANES):
            cols = slice(512 * j + LANES * c, 512 * j + LANES * (c + 1))
            act_ref[0, :, cols] = co_scr[c].astype(BF16)

    def z_block(j):
        v = seg(u, 512 * j, 512)
        if j == 0:
            zf_ref[0] = v.astype(BF16)
        elif j == 1:
            for c in range(512 // LANES):
                uf_ref[0, c] = v[:, LANES * c : LANES * (c + 1)]
        else:
            zs_ref[0, :, 512 * (j - 2) : 512 * (j - 1)] = v.astype(BF16)

    for j in range(CONV_DIM // 512):
        conv_block(j)
        z_block(j)
    dt_pre = seg(u, DT_COL, LANES) + dtb_ref[...]
    dt_ref[0] = jnp.maximum(dt_pre, 0.0) + jnp.log1p(jnp.exp(-jnp.abs(dt_pre)))


def _in_proj(x, norm_w, w_in_b, conv_w, conv_b, dt_bias_p, tm):
    b, s, _ = x.shape
    tile = lambda w: pl.BlockSpec((1, tm, w), lambda bi, i: (bi, i, 0))
    fixed = lambda bi, i: (0, 0)
    outs = [(512, BF16), None, (D_SSM, BF16), (CONV_DIM, BF16), (LANES, F32)]
    out_specs = [tile(o[0]) if o else None for o in outs]
    out_shape = [jax.ShapeDtypeStruct((b, s, o[0]), o[1]) if o else None for o in outs]
    out_specs[1] = pl.BlockSpec((1, 512 // LANES, tm, LANES), lambda bi, i: (bi, 0, i, 0))
    out_shape[1] = jax.ShapeDtypeStruct((b, 512 // LANES, s, LANES), F32)
    return pl.pallas_call(
        functools.partial(_in_proj_kernel, tm=tm),
        grid=(b, s // tm),
        in_specs=[
            pl.BlockSpec(
                (1, HALO, D_MODEL),
                lambda bi, i: (bi, jnp.maximum(i * (tm // HALO) - 1, 0), 0),
            ),
            tile(D_MODEL),
            pl.BlockSpec(
                (1, HALO, D_MODEL),
                lambda bi, i: (bi, jnp.minimum((i + 1) * (tm // HALO), s // HALO - 1), 0),
            ),
            pl.BlockSpec((1, D_MODEL), fixed),
            pl.BlockSpec((D_MODEL, D_IN_PAD), fixed),
            pl.BlockSpec((D_CONV, CONV_DIM), fixed),
            pl.BlockSpec((1, CONV_DIM), fixed),
            pl.BlockSpec((1, LANES), fixed),
        ],
        out_specs=out_specs,
        out_shape=out_shape,
        scratch_shapes=[
            pltpu.VMEM((512 // LANES, tm + 2 * HALO, LANES), F32),
            pltpu.VMEM((512 // LANES, tm, LANES), F32),
        ],
        compiler_params=_cparams(2),
        name="in_proj",
    )(x, x, x, norm_w, w_in_b, conv_w, conv_b, dt_bias_p)


def _fft1_kernel(u_ref, w1_ref, tc_ref, ts_ref, re_ref, mi_ref):
    n1 = re_ref.shape[2]
    j = pl.program_id(1)
    w1 = w1_ref[...]
    for q in range(FFT_S2_STEP):
        s2 = FFT_S2_STEP * j + q
        x = jnp.concatenate(
            [u_ref[0, c, pl.ds(s2, n1, stride=FFT_N2), :] for c in range(512 // LANES)], axis=1
        )
        pq = jnp.dot(w1, x.astype(BF16), preferred_element_type=F32)
        p = pq[:n1]
        qm = pq[n1:]
        c = jnp.concatenate([tc_ref[q]] * 4, axis=1)
        s = jnp.concatenate([ts_ref[q]] * 4, axis=1)
        re_ref[0, q] = (p * c - qm * s).astype(BF16)
        mi_ref[0, q] = (p * s + qm * c).astype(BF16)


def _fft3_kernel(re_ref, mi_ref, k3_ref, cs_ref, wbd_ref, zf_ref, y_ref, mix_scr, *, scale):
    @pl.when((pl.program_id(0) == 0) & (pl.program_id(1) == 0))
    def _():
        mix = jnp.dot(cs_ref[...], wbd_ref[...], preferred_element_type=F32) * scale
        mix_scr[...] = mix.astype(BF16)

    kt = FFT_K1_TILE
    rows = FFT_N2 * kt
    zs = []
    for t in range(FFT_K1_STEP // kt):
        sl = slice(kt * t, kt * (t + 1))
        rhs = jnp.concatenate(
            [re_ref[0, :, sl, :].reshape(rows, 512), mi_ref[0, :, sl, :].reshape(rows, 512)],
            axis=0,
        )
        z = jnp.dot(k3_ref[...], rhs, preferred_element_type=F32)
        zs.append(jnp.concatenate([z[:rows], z[rows:]], axis=1).astype(BF16))
    y = jnp.dot(jnp.concatenate(zs, axis=0), mix_scr[...], preferred_element_type=F32)
    for t in range(FFT_K1_STEP // kt):
        sl = slice(kt * t, kt * (t + 1))
        zf = zf_ref[0, :, sl, :].reshape(rows, 512).astype(F32)
        yt = y[rows * t : rows * (t + 1)] * _silu(zf)
        y_ref[0, :, sl, :] = yt.astype(BF16).reshape(FFT_N2, kt, 512)


def _dft_cos_sin(n):
    j = np.arange(n, dtype=np.int64)
    ang = 2.0 * np.pi * ((j[:, None] * j[None, :]) % n).astype(np.float64) / n
    return np.cos(ang), np.sin(ang)


@functools.lru_cache(maxsize=None)
def _fft_constants(s):
    n2 = FFT_N2
    n1 = s // n2
    c1, s1 = _dft_cos_sin(n1)
    w1 = np.concatenate([c1, s1], axis=0)
    k1 = np.arange(n1, dtype=np.int64)
    s2 = np.arange(n2, dtype=np.int64)
    ang = 2.0 * np.pi * ((s2[:, None] * k1[None, :]) % s).astype(np.float64) / s
    tc = np.repeat(np.cos(ang)[:, :, None], LANES, axis=2)
    ts = np.repeat(np.sin(ang)[:, :, None], LANES, axis=2)
    c2, s2m = _dft_cos_sin(n2)
    eye = np.eye(FFT_K1_TILE)
    kc = np.kron(c2, eye)
    ks = np.kron(s2m, eye)
    k3 = np.block([[kc, -ks], [ks, kc]])
    cc, sc = _dft_cos_sin(F_GROUP)
    eye_g = np.eye(N_FGROUPS)
    cs = np.concatenate([np.kron(eye_g, cc), -np.kron(eye_g, sc)], axis=0)
    return (
        jnp.asarray(w1, BF16),
        jnp.asarray(tc, F32),
        jnp.asarray(ts, F32),
        jnp.asarray(k3, BF16),
        jnp.asarray(cs, BF16),
    )


def _fnet(u_f4, z_f, wbd_b):
    b, _, s, _ = u_f4.shape
    n2 = FFT_N2
    n1 = s // n2
    assert s == n1 * n2 and n1 % FFT_K1_STEP == 0, s
    w1, tc, ts, k3, cs = _fft_constants(s)
    a_shape = jax.ShapeDtypeStruct((b, n2, n1, 512), BF16)
    q = FFT_S2_STEP
    re, mi = pl.pallas_call(
        _fft1_kernel,
        grid=(b, n2 // q),
        in_specs=[
            pl.BlockSpec((1, 512 // LANES, s, LANES), lambda i, j: (i, 0, 0, 0)),
            pl.BlockSpec((2 * n1, n1), lambda i, j: (0, 0)),
            pl.BlockSpec((q, n1, LANES), lambda i, j: (j, 0, 0)),
            pl.BlockSpec((q, n1, LANES), lambda i, j: (j, 0, 0)),
        ],
        out_specs=[
            pl.BlockSpec((1, q, n1, 512), lambda i, j: (i, j, 0, 0)),
            pl.BlockSpec((1, q, n1, 512), lambda i, j: (i, j, 0, 0)),
        ],
        out_shape=[a_shape, a_shape],
        compiler_params=_cparams(2),
        name="fft1",
    )(u_f4, w1, tc, ts)

    blk = pl.BlockSpec((1, n2, FFT_K1_STEP, 512), lambda i, j: (i, 0, j, 0))
    fixed = lambda i, j: (0, 0)
    scale = 1.0 / math.sqrt(float(s) * F_GROUP)
    y4 = pl.pallas_call(
        functools.partial(_fft3_kernel, scale=scale),
        grid=(b, n1 // FFT_K1_STEP),
        in_specs=[
            blk,
            blk,
            pl.BlockSpec(k3.shape, fixed),
            pl.BlockSpec(cs.shape, fixed),
            pl.BlockSpec((512, 512), fixed),
            blk,
        ],
        out_specs=blk,
        out_shape=jax.ShapeDtypeStruct((b, n2, n1, 512), BF16),
        scratch_shapes=[pltpu.VMEM(cs.shape, BF16)],
        compiler_params=_cparams(2),
        name="fft3",
    )(re, mi, k3, cs, wbd_b, z_f.reshape(b, n2, n1, 512))
    return y4.reshape(b, s, 512)


def _split3_dot(tri, v):
    v1 = v.astype(BF16)
    r1 = v - v1.astype(F32)
    v2 = r1.astype(BF16)
    v3 = (r1 - v2.astype(F32)).astype(BF16)
    d = lambda w: jnp.dot(tri, w, preferred_element_type=F32)
    return d(v1) + d(v2) + d(v3)


N_PANELS = 4
LOG2E = math.log2(math.e)


def _ssd_panels(dt, alog, tri, pan_ref, etot_ref, c, *, forward):
    a = dt * (-jnp.exp(alog))
    acs = _split3_dot(tri, a)
    total = acs[CHUNK - 1 : CHUNK, :] if forward else acs[0:1, :]
    acs2 = acs * LOG2E
    l2dt = jnp.log2(dt)
    pan_ref[c, 0] = acs2
    pan_ref[c, 1] = (acs2 - l2dt).T
    pan_ref[c, 2] = l2dt.T
    pan_ref[c, 3] = (dt * jnp.exp(total - acs)).T
    etot_ref[c] = jnp.broadcast_to(jnp.exp(total), (8, LANES))


def _ssd_chunk(act_ref, r0, pan_ref, etot_ref, ci, state_ref, gb_ref, emit, *, lane0, forward):
    rows = pl.ds(r0, CHUNK)
    e_tot = etot_ref[ci][0:1, :]

    li = lax.broadcasted_iota(jnp.int32, (CHUNK, CHUNK), 0)
    si = lax.broadcasted_iota(jnp.int32, (CHUNK, CHUNK), 1)
    keep = (li >= si) if forward else (li <= si)
    left = lax.broadcasted_iota(jnp.int32, (CHUNK, LANES), 1) < HEAD_DIM
    left_row = lax.broadcasted_iota(jnp.int32, (1, LANES), 1) < HEAD_DIM
    nt_dims = (((1,), (1,)), ((), ()))

    for g in range(N_BC_GROUPS):
        bg = act_ref[0, rows, pl.ds(D_SSM + D_STATE * g, D_STATE)]
        cg = act_ref[0, rows, pl.ds(D_SSM + 512 + D_STATE * g, D_STATE)]
        cb = lax.dot_general(cg, bg, nt_dims, preferred_element_type=F32)
        gb_ref[0] = jnp.where(keep, cb, 0.0)
        gb_ref[1] = bg.astype(F32).T
        y_off = jnp.dot(cg, state_ref[g].astype(BF16), preferred_element_type=F32)
        ys, sts = [], []
        for j in range(HEADS_PER_GROUP // 2):
            pair = (HEADS_PER_GROUP // 2) * g + j
            psl = slice(LANES * j, LANES * (j + 1))
            xp = act_ref[0, rows, pl.ds(LANES * pair, LANES)]
            zero = jnp.zeros_like(xp)
            xbd = jnp.concatenate([jnp.where(left, xp, zero), jnp.where(left, zero, xp)], axis=0)
            gl, wl, es, et = [], [], [], []
            for h in (2 * pair, 2 * pair + 1):
                hl = lane0 + h
                row = lambda p: pan_ref[ci, p, pl.ds(hl, 1), :]
                colb = jnp.broadcast_to(pan_ref[ci, 0][:, hl : hl + 1], (CHUNK, LANES))
                dec = jnp.exp2(jnp.minimum(colb - row(1), row(2)))
                gl.append((dec * gb_ref[0]).astype(BF16))
                wl.append((gb_ref[1] * row(3)).astype(BF16))
                es.append(jnp.exp2(colb))
                et.append(jnp.broadcast_to(e_tot[:, hl : hl + 1], (1, LANES)))
            lhs = jnp.concatenate(
                [jnp.concatenate(gl, axis=1), jnp.concatenate(wl, axis=1)], axis=0
            )
            res = jnp.dot(lhs, xbd, preferred_element_type=F32)
            ys.append(res[:CHUNK] + y_off[:, psl] * jnp.where(left, es[0], es[1]))
            sts.append(
                state_ref[g, :, psl] * jnp.where(left_row, et[0], et[1]) + res[CHUNK:]
            )
        state_ref[g] = jnp.concatenate(sts, axis=1)
        emit(g, jnp.concatenate(ys, axis=1))


def _ssd_fwd_kernel(act_ref, dt_ref, alog_ref, tri_ref, yf_ref, state_scr, pan_scr, etot_scr,
                    gb_scr, *, tm):
    @pl.when(pl.program_id(1) == 0)
    def _():
        state_scr[...] = jnp.zeros_like(state_scr)

    for c in range(tm // CHUNK):
        _ssd_panels(dt_ref[0, CHUNK * c : CHUNK * (c + 1), :], alog_ref[...], tri_ref[...],
                    pan_scr, etot_scr, c, forward=True)

    def body(ci, carry):
        r0 = pl.multiple_of(ci * CHUNK, CHUNK)

        def emit(g, y):
            yf_ref[0, pl.ds(r0, CHUNK), pl.ds(GROUP_W * g, GROUP_W)] = y

        _ssd_chunk(act_ref, r0, pan_scr, etot_scr, ci, state_scr, gb_scr, emit, lane0=0,
                   forward=True)
        return carry

    lax.fori_loop(0, tm // CHUNK, body, 0, unroll=SSD_FWD_UNROLL)


def _ssd_bwd_kernel(act_ref, dt_ref, yf_ref, zs_ref, alog_ref, tri_ref, dskip_ref,
                    nw_ref, ys_ref, state_scr, pan_scr, etot_scr, gb_scr, *, tm):
    @pl.when(pl.program_id(1) == 0)
    def _():
        state_scr[...] = jnp.zeros_like(state_scr)

    nc = tm // CHUNK
    for c in range(nc):
        _ssd_panels(dt_ref[0, CHUNK * c : CHUNK * (c + 1), :], alog_ref[...], tri_ref[...],
                    pan_scr, etot_scr, c, forward=False)

    def body(cj, carry):
        ci = nc - 1 - cj
        r0 = pl.multiple_of(ci * CHUNK, CHUNK)
        rows = pl.ds(r0, CHUNK)

        def emit(g, y):
            gsl = pl.ds(GROUP_W * g, GROUP_W)
            xs = act_ref[0, rows, gsl].astype(F32)
            y = y + yf_ref[0, rows, gsl] + dskip_ref[:, gsl] * xs
            y = y * _silu(zs_ref[0, rows, gsl].astype(F32))
            ms = jnp.mean(y * y, axis=-1, keepdims=True)
            ys_ref[0, rows, gsl] = ((y * lax.rsqrt(ms + EPS)) * nw_ref[:, gsl]).astype(BF16)

        _ssd_chunk(act_ref, r0, pan_scr, etot_scr, ci, state_scr, gb_scr, emit, lane0=N_HEADS,
                   forward=False)
        return carry

    lax.fori_loop(0, nc, body, 0, unroll=SSD_BWD_UNROLL)


@functools.lru_cache(maxsize=None)
def _ssd_constants():
    tri_f = np.tril(np.ones((CHUNK, CHUNK)))
    tri_b = np.triu(np.ones((CHUNK, CHUNK)))
    return jnp.asarray(tri_f, BF16), jnp.asarray(tri_b, BF16)


def _ssd(act, dt, z_s, alog_p, dskip_x, ssd_nw, tm):
    b, s, _ = act.shape
    nt = s // tm
    tri_f, tri_b = _ssd_constants()
    fixed = lambda bi, i: (0, 0)
    tile = lambda w: pl.BlockSpec((1, tm, w), lambda bi, i: (bi, i, 0))
    scratch = [
        pltpu.VMEM((N_BC_GROUPS, D_STATE, GROUP_W), F32),
        pltpu.VMEM((tm // CHUNK, N_PANELS, CHUNK, LANES), F32),
        pltpu.VMEM((tm // CHUNK, 8, LANES), F32),
        pltpu.VMEM((2, CHUNK, CHUNK), F32),
    ]

    y_fwd = pl.pallas_call(
        functools.partial(_ssd_fwd_kernel, tm=tm),
        grid=(b, nt),
        in_specs=[
            tile(CONV_DIM),
            tile(LANES),
            pl.BlockSpec((1, LANES), fixed),
            pl.BlockSpec((CHUNK, CHUNK), fixed),
        ],
        out_specs=tile(D_SSM),
        out_shape=jax.ShapeDtypeStruct((b, s, D_SSM), F32),
        scratch_shapes=scratch,
        compiler_params=_cparams(2),
        name="ssd_fwd",
    )(act, dt, alog_p, tri_f)

    rtile = lambda w: pl.BlockSpec((1, tm, w), lambda bi, i: (bi, nt - 1 - i, 0))
    return pl.pallas_call(
        functools.partial(_ssd_bwd_kernel, tm=tm),
        grid=(b, nt),
        in_specs=[
            rtile(CONV_DIM),
            rtile(LANES),
            rtile(D_SSM),
            rtile(D_SSM),
            pl.BlockSpec((1, LANES), fixed),
            pl.BlockSpec((CHUNK, CHUNK), fixed),
            pl.BlockSpec((1, D_SSM), fixed),
            pl.BlockSpec((1, D_SSM), fixed),
        ],
        out_specs=rtile(D_SSM),
        out_shape=jax.ShapeDtypeStruct((b, s, D_SSM), BF16),
        scratch_shapes=scratch,
        compiler_params=_cparams(2),
        name="ssd_bwd",
    )(act, dt, y_fwd, z_s, alog_p, tri_b, dskip_x, ssd_nw)


def _epilogue_kernel(x_ref, yf_ref, ys_ref, p_ref, wo_ref, wg_ref, wp_ref, fw_ref, o_ref):
    mix = jnp.concatenate([yf_ref[...], ys_ref[...]], axis=1)
    h = x_ref[...] + jnp.dot(mix, wo_ref[...], preferred_element_type=F32)
    gate = _sigmoid(jnp.dot(h.astype(BF16), wg_ref[...], preferred_element_type=F32))
    ple = jnp.dot(p_ref[...].astype(BF16), wp_ref[...], preferred_element_type=F32)
    h = h + ple * gate
    ms = jnp.mean(h * h, axis=-1, keepdims=True)
    o_ref[...] = (h * lax.rsqrt(ms + EPS)) * fw_ref[...]


def _epilogue(x2, y_f, y_s, p2, w_out_b, w_gate_b, w_ple_b, final_w, tm):
    t = x2.shape[0]
    row = lambda i: (i, 0)
    fixed = lambda i: (0, 0)
    return pl.pallas_call(
        _epilogue_kernel,
        grid=(t // tm,),
        in_specs=[
            pl.BlockSpec((tm, D_MODEL), row),
            pl.BlockSpec((tm, 512), row),
            pl.BlockSpec((tm, D_SSM), row),
            pl.BlockSpec((tm, D_PLE), row),
            pl.BlockSpec((2048, D_MODEL), fixed),
            pl.BlockSpec((D_MODEL, D_MODEL), fixed),
            pl.BlockSpec((D_PLE, D_MODEL), fixed),
            pl.BlockSpec((1, D_MODEL), fixed),
        ],
        out_specs=pl.BlockSpec((tm, D_MODEL), row),
        out_shape=jax.ShapeDtypeStruct((t, D_MODEL), F32),
        compiler_params=_cparams(1),
        name="epilogue",
    )(x2, y_f, y_s, p2, w_out_b, w_gate_b, w_ple_b, final_w)


def _trunk(x, p, norm_w, w_in, w_fmix, conv_w, conv_b, a_log_f, a_log_b, dt_bias_f,
           dt_bias_b, d_skip, ssd_norm_w, w_out, w_ple_in, w_ple_gate, final_norm_w):
    b, s, _ = x.shape
    t = b * s
    tm = 512
    assert s % tm == 0, s

    pad = jnp.zeros((LANES - 2 * N_HEADS,), F32)
    bias_p = jnp.concatenate([dt_bias_f[0], dt_bias_b[0], pad])[None, :]
    alog_p = jnp.concatenate([a_log_f[0], a_log_b[0], pad])[None, :]
    dskip_x = jnp.repeat(d_skip[0], HEAD_DIM)[None, :]

    w_in_b = jnp.pad(w_in[0], ((0, 0), (0, D_IN_PAD - D_IN_PROJ))).astype(BF16)
    z_f, u_f, z_s, act, dt = _in_proj(
        x, norm_w[0][None, :], w_in_b, conv_w[0], conv_b[0][None, :], bias_p, tm
    )

    wbd = jax.scipy.linalg.block_diag(*[w_fmix[0, g] for g in range(N_FGROUPS)]).astype(BF16)
    y_f = _fnet(u_f, z_f, wbd)

    y_s = _ssd(act, dt, z_s, alog_p, dskip_x, ssd_norm_w[0][None, :], tm)

    out = _epilogue(
        x.reshape(t, D_MODEL), y_f.reshape(t, 512), y_s.reshape(t, D_SSM), p[0].reshape(t, D_PLE),
        w_out[0].astype(BF16), w_ple_gate[0].astype(BF16), w_ple_in[0].astype(BF16),
        final_norm_w[None, :], tm,
    )
    return out.reshape(b, s, D_MODEL)


def kernel(x_prompt, x_sample, p_prompt, p_sample, norm_w, w_in, w_fmix, conv_w, conv_b, a_log_f, a_log_b, dt_bias_f, dt_bias_b, d_skip, ssd_norm_w, w_out, w_ple_in, w_ple_gate, final_norm_w):
    weights = (norm_w, w_in, w_fmix, conv_w, conv_b, a_log_f, a_log_b, dt_bias_f, dt_bias_b,
               d_skip, ssd_norm_w, w_out, w_ple_in, w_ple_gate, final_norm_w)
    return (_trunk(x_prompt, p_prompt, *weights), _trunk(x_sample, p_sample, *weights))
```

```python
import functools
import math

import jax
import jax.numpy as jnp
import numpy as np
from jax import lax
from jax.experimental import pallas as pl
from jax.experimental.pallas import tpu as pltpu

F32 = jnp.float32
BF16 = jnp.bfloat16

D_MODEL = 1024
D_FOURIER = 512
N_FGROUPS = 8
F_GROUP = 64
D_SSM = 1536
HEAD_DIM = 64
N_HEADS = 24
N_BC_GROUPS = 4
HEADS_PER_GROUP = 6
GROUP_W = HEADS_PER_GROUP * HEAD_DIM
D_STATE = 128
D_CONV = 5
CONV_DIM = 2560
CHUNK = 128
D_PLE = 256
D_IN_PROJ = 5168
D_IN_PAD = 5248
DT_COL = 5120
EPS = 1e-6

LANES = 128
HALO = 8
SSD_FWD_UNROLL = 4
SSD_BWD_UNROLL = 2
FFT_N2 = 16
FFT_K1_TILE = 16
FFT_K1_STEP = 64
FFT_SLABS = 2
VMEM_LIMIT = 56 * 1024 * 1024


def _cparams(n_axes):
    return pltpu.CompilerParams(
        dimension_semantics=("arbitrary",) * n_axes, vmem_limit_bytes=VMEM_LIMIT
    )


def _sigmoid(v):
    return 0.5 + 0.5 * jnp.tanh(0.5 * v)


def _silu(v):
    hv = 0.5 * v
    return hv + hv * jnp.tanh(hv)


def _in_proj_kernel(xp_ref, x_ref, xn_ref, nw_ref, w_ref, cw_ref, cb_ref, dtb_ref,
                    zf_ref, uf_ref, zs_ref, act_ref, dt_ref, xe_scr, co_scr, *, tm):
    i = pl.program_id(1)
    nt = pl.num_programs(1)

    def norm(x):
        ms = jnp.mean(x * x, axis=-1, keepdims=True)
        return (x * lax.rsqrt(ms + EPS)) * nw_ref[...]

    u32 = norm(x_ref[0])
    up = jnp.where(i == 0, 0.0, norm(xp_ref[0]))
    un = jnp.where(i == nt - 1, 0.0, norm(xn_ref[0]))
    u = u32.astype(BF16)
    u_ext = jnp.concatenate([up, u32, un], axis=0).astype(BF16)

    def seg(lhs, c0, width):
        return jnp.dot(lhs, w_ref[:, c0 : c0 + width], preferred_element_type=F32)

    mid = D_CONV // 2

    def conv_block(j):
        xe = seg(u_ext, 2560 + 512 * j, 512)
        for c in range(512 // LANES):
            xe_scr[c] = xe[:, LANES * c : LANES * (c + 1)]
        for c in range(512 // LANES):
            cols = slice(512 * j + LANES * c, 512 * j + LANES * (c + 1))
            for par in range(2):
                acc = cb_ref[:, cols]
                for k in range(D_CONV):
                    tap = xe_scr[c, pl.ds(HALO - mid + k + par, tm // 2, stride=2), :]
                    acc = acc + tap * cw_ref[k : k + 1, cols]
                co_scr[c, pl.ds(par, tm // 2, stride=2), :] = _silu(acc)
        for c in range(512 // LANES):
            cols = slice(512 * j + LANES * c, 512 * j + LANES * (c + 1))
            act_ref[0, :, cols] = co_scr[c].astype(BF16)

    def z_block(j):
        v = seg(u, 512 * j, 512)
        if j == 0:
            zf_ref[0] = v.astype(BF16)
        elif j == 1:
            for c in range(512 // LANES):
                uf_ref[0, c] = v[:, LANES * c : LANES * (c + 1)]
        else:
            zs_ref[0, :, 512 * (j - 2) : 512 * (j - 1)] = v.astype(BF16)

    for j in range(CONV_DIM // 512):
        conv_block(j)
        z_block(j)
    dt_pre = seg(u, DT_COL, LANES) + dtb_ref[...]
    dt_ref[0] = jnp.maximum(dt_pre, 0.0) + jnp.log1p(jnp.exp(-jnp.abs(dt_pre)))


def _in_proj(x, norm_w, w_in_b, conv_w, conv_b, dt_bias_p, tm):
    b, s, _ = x.shape
    tile = lambda w: pl.BlockSpec((1, tm, w), lambda bi, i: (bi, i, 0))
    fixed = lambda bi, i: (0, 0)
    outs = [(512, BF16), None, (D_SSM, BF16), (CONV_DIM, BF16), (LANES, F32)]
    out_specs = [tile(o[0]) if o else None for o in outs]
    out_shape = [jax.ShapeDtypeStruct((b, s, o[0]), o[1]) if o else None for o in outs]
    out_specs[1] = pl.BlockSpec((1, 512 // LANES, tm, LANES), lambda bi, i: (bi, 0, i, 0))
    out_shape[1] = jax.ShapeDtypeStruct((b, 512 // LANES, s, LANES), F32)
    return pl.pallas_call(
        functools.partial(_in_proj_kernel, tm=tm),
        grid=(b, s // tm),
        in_specs=[
            pl.BlockSpec(
                (1, HALO, D_MODEL),
                lambda bi, i: (bi, jnp.maximum(i * (tm // HALO) - 1, 0), 0),
            ),
            tile(D_MODEL),
            pl.BlockSpec(
                (1, HALO, D_MODEL),
                lambda bi, i: (bi, jnp.minimum((i + 1) * (tm // HALO), s // HALO - 1), 0),
            ),
            pl.BlockSpec((1, D_MODEL), fixed),
            pl.BlockSpec((D_MODEL, D_IN_PAD), fixed),
            pl.BlockSpec((D_CONV, CONV_DIM), fixed),
            pl.BlockSpec((1, CONV_DIM), fixed),
            pl.BlockSpec((1, LANES), fixed),
        ],
        out_specs=out_specs,
        out_shape=out_shape,
        scratch_shapes=[
            pltpu.VMEM((512 // LANES, tm + 2 * HALO, LANES), F32),
            pltpu.VMEM((512 // LANES, tm, LANES), F32),
        ],
        compiler_params=_cparams(2),
        name="in_proj",
    )(x, x, x, norm_w, w_in_b, conv_w, conv_b, dt_bias_p)


def _fft1_kernel(u_ref, w1_ref, tc_ref, ts_ref, re_ref, mi_ref):
    n1 = re_ref.shape[2]
    w1 = w1_ref[...]
    for s2 in range(FFT_N2):
        x = jnp.concatenate(
            [u_ref[0, c, pl.ds(s2, n1, stride=FFT_N2), :] for c in range(FFT_SLABS)], axis=1
        )
        pq = jnp.dot(w1, x.astype(BF16), preferred_element_type=F32)
        p = pq[:n1]
        qm = pq[n1:]
        c = jnp.concatenate([tc_ref[s2]] * FFT_SLABS, axis=1)
        s = jnp.concatenate([ts_ref[s2]] * FFT_SLABS, axis=1)
        re_ref[0, s2] = (p * c - qm * s).astype(BF16)
        mi_ref[0, s2] = (p * s + qm * c).astype(BF16)


def _fft3_kernel(re_ref, mi_ref, k3_ref, cs_ref, wbd_ref, zf_ref, y_ref, mix_scr, *, scale):
    @pl.when((pl.program_id(0) == 0) & (pl.program_id(1) == 0))
    def _():
        mix = jnp.dot(cs_ref[...], wbd_ref[...], preferred_element_type=F32) * scale
        mix_scr[...] = mix.astype(BF16)

    kt = FFT_K1_TILE
    rows = FFT_N2 * kt
    zs = []
    for t in range(FFT_K1_STEP // kt):
        sl = slice(kt * t, kt * (t + 1))
        rhs = jnp.concatenate(
            [re_ref[0, :, sl, :].reshape(rows, 512), mi_ref[0, :, sl, :].reshape(rows, 512)],
            axis=0,
        )
        z = jnp.dot(k3_ref[...], rhs, preferred_element_type=F32)
        zs.append(jnp.concatenate([z[:rows], z[rows:]], axis=1).astype(BF16))
    y = jnp.dot(jnp.concatenate(zs, axis=0), mix_scr[...], preferred_element_type=F32)
    for t in range(FFT_K1_STEP // kt):
        sl = slice(kt * t, kt * (t + 1))
        zf = zf_ref[0, :, sl, :].reshape(rows, 512).astype(F32)
        yt = y[rows * t : rows * (t + 1)] * _silu(zf)
        y_ref[0, :, sl, :] = yt.astype(BF16).reshape(FFT_N2, kt, 512)


def _dft_cos_sin(n):
    j = np.arange(n, dtype=np.int64)
    ang = 2.0 * np.pi * ((j[:, None] * j[None, :]) % n).astype(np.float64) / n
    return np.cos(ang), np.sin(ang)


@functools.lru_cache(maxsize=None)
def _fft_constants(s):
    n2 = FFT_N2
    n1 = s // n2
    c1, s1 = _dft_cos_sin(n1)
    w1 = np.concatenate([c1, s1], axis=0)
    k1 = np.arange(n1, dtype=np.int64)
    s2 = np.arange(n2, dtype=np.int64)
    ang = 2.0 * np.pi * ((s2[:, None] * k1[None, :]) % s).astype(np.float64) / s
    tc = np.repeat(np.cos(ang)[:, :, None], LANES, axis=2)
    ts = np.repeat(np.sin(ang)[:, :, None], LANES, axis=2)
    c2, s2m = _dft_cos_sin(n2)
    eye = np.eye(FFT_K1_TILE)
    kc = np.kron(c2, eye)
    ks = np.kron(s2m, eye)
    k3 = np.block([[kc, -ks], [ks, kc]])
    cc, sc = _dft_cos_sin(F_GROUP)
    eye_g = np.eye(N_FGROUPS)
    cs = np.concatenate([np.kron(eye_g, cc), -np.kron(eye_g, sc)], axis=0)
    return (
        jnp.asarray(w1, BF16),
        jnp.asarray(tc, F32),
        jnp.asarray(ts, F32),
        jnp.asarray(k3, BF16),
        jnp.asarray(cs, BF16),
    )


def _fnet(u_f4, z_f, wbd_b):
    b, _, s, _ = u_f4.shape
    n2 = FFT_N2
    n1 = s // n2
    assert s == n1 * n2 and n1 % FFT_K1_STEP == 0, s
    w1, tc, ts, k3, cs = _fft_constants(s)
    a_shape = jax.ShapeDtypeStruct((b, n2, n1, 512), BF16)
    re, mi = pl.pallas_call(
        _fft1_kernel,
        grid=(b, 512 // (FFT_SLABS * LANES)),
        in_specs=[
            pl.BlockSpec((1, FFT_SLABS, s, LANES), lambda i, j: (i, j, 0, 0)),
            pl.BlockSpec((2 * n1, n1), lambda i, j: (0, 0)),
            pl.BlockSpec((n2, n1, LANES), lambda i, j: (0, 0, 0)),
            pl.BlockSpec((n2, n1, LANES), lambda i, j: (0, 0, 0)),
        ],
        out_specs=[
            pl.BlockSpec((1, n2, n1, FFT_SLABS * LANES), lambda i, j: (i, 0, 0, j)),
            pl.BlockSpec((1, n2, n1, FFT_SLABS * LANES), lambda i, j: (i, 0, 0, j)),
        ],
        out_shape=[a_shape, a_shape],
        compiler_params=_cparams(2),
        name="fft1",
    )(u_f4, w1, tc, ts)

    blk = pl.BlockSpec((1, n2, FFT_K1_STEP, 512), lambda i, j: (i, 0, j, 0))
    fixed = lambda i, j: (0, 0)
    scale = 1.0 / math.sqrt(float(s) * F_GROUP)
    y4 = pl.pallas_call(
        functools.partial(_fft3_kernel, scale=scale),
        grid=(b, n1 // FFT_K1_STEP),
        in_specs=[
            blk,
            blk,
            pl.BlockSpec(k3.shape, fixed),
            pl.BlockSpec(cs.shape, fixed),
            pl.BlockSpec((512, 512), fixed),
            blk,
        ],
        out_specs=blk,
        out_shape=jax.ShapeDtypeStruct((b, n2, n1, 512), BF16),
        scratch_shapes=[pltpu.VMEM(cs.shape, BF16)],
        compiler_params=_cparams(2),
        name="fft3",
    )(re, mi, k3, cs, wbd_b, z_f.reshape(b, n2, n1, 512))
    return y4.reshape(b, s, 512)


def _split3_dot(tri, v):
    v1 = v.astype(BF16)
    r1 = v - v1.astype(F32)
    v2 = r1.astype(BF16)
    v3 = (r1 - v2.astype(F32)).astype(BF16)
    d = lambda w: jnp.dot(tri, w, preferred_element_type=F32)
    return d(v1) + d(v2) + d(v3)


N_PANELS = 4
LOG2E = math.log2(math.e)


def _ssd_panels(dt, alog, tri, pan_ref, etot_ref, c, *, forward):
    a = dt * (-jnp.exp(alog))
    acs = _split3_dot(tri, a)
    total = acs[CHUNK - 1 : CHUNK, :] if forward else acs[0:1, :]
    acs2 = acs * LOG2E
    l2dt = jnp.log2(dt)
    pan_ref[c, 0] = acs2
    pan_ref[c, 1] = (acs2 - l2dt).T
    pan_ref[c, 2] = l2dt.T
    pan_ref[c, 3] = (dt * jnp.exp(total - acs)).T
    etot_ref[c] = jnp.broadcast_to(jnp.exp(total), (8, LANES))


def _ssd_chunk(act_ref, r0, pan_ref, etot_ref, ci, state_ref, gb_ref, emit, *, lane0, forward):
    rows = pl.ds(r0, CHUNK)
    e_tot = etot_ref[ci][0:1, :]

    li = lax.broadcasted_iota(jnp.int32, (CHUNK, CHUNK), 0)
    si = lax.broadcasted_iota(jnp.int32, (CHUNK, CHUNK), 1)
    keep = (li >= si) if forward else (li <= si)
    left = lax.broadcasted_iota(jnp.int32, (CHUNK, LANES), 1) < HEAD_DIM
    left_row = lax.broadcasted_iota(jnp.int32, (1, LANES), 1) < HEAD_DIM
    nt_dims = (((1,), (1,)), ((), ()))

    for g in range(N_BC_GROUPS):
        bg = act_ref[0, rows, pl.ds(D_SSM + D_STATE * g, D_STATE)]
        cg = act_ref[0, rows, pl.ds(D_SSM + 512 + D_STATE * g, D_STATE)]
        cb = lax.dot_general(cg, bg, nt_dims, preferred_element_type=F32)
        gb_ref[0] = jnp.where(keep, cb, 0.0)
        gb_ref[1] = bg.astype(F32).T
        y_off = jnp.dot(cg, state_ref[g].astype(BF16), preferred_element_type=F32)
        ys, sts = [], []
        for j in range(HEADS_PER_GROUP // 2):
            pair = (HEADS_PER_GROUP // 2) * g + j
            psl = slice(LANES * j, LANES * (j + 1))
            xp = act_ref[0, rows, pl.ds(LANES * pair, LANES)]
            zero = jnp.zeros_like(xp)
            xbd = jnp.concatenate([jnp.where(left, xp, zero), jnp.where(left, zero, xp)], axis=0)
            gl, wl, es, et = [], [], [], []
            for h in (2 * pair, 2 * pair + 1):
                hl = lane0 + h
                row = lambda p: pan_ref[ci, p, pl.ds(hl, 1), :]
                colb = jnp.broadcast_to(pan_ref[ci, 0][:, hl : hl + 1], (CHUNK, LANES))
                dec = jnp.exp2(jnp.minimum(colb - row(1), row(2)))
                gl.append((dec * gb_ref[0]).astype(BF16))
                wl.append((gb_ref[1] * row(3)).astype(BF16))
                es.append(colb)
                et.append(jnp.broadcast_to(e_tot[:, hl : hl + 1], (1, LANES)))
            lhs = jnp.concatenate(
                [jnp.concatenate(gl, axis=1), jnp.concatenate(wl, axis=1)], axis=0
            )
            res = jnp.dot(lhs, xbd, preferred_element_type=F32)
            ys.append(res[:CHUNK] + y_off[:, psl] * jnp.exp2(jnp.where(left, es[0], es[1])))
            sts.append(
                state_ref[g, :, psl] * jnp.where(left_row, et[0], et[1]) + res[CHUNK:]
            )
        state_ref[g] = jnp.concatenate(sts, axis=1)
        emit(g, jnp.concatenate(ys, axis=1))


def _ssd_fwd_kernel(act_ref, dt_ref, alog_ref, tri_ref, yf_ref, state_scr, pan_scr, etot_scr,
                    gb_scr, *, tm):
    @pl.when(pl.program_id(1) == 0)
    def _():
        state_scr[...] = jnp.zeros_like(state_scr)

    for c in range(tm // CHUNK):
        _ssd_panels(dt_ref[0, CHUNK * c : CHUNK * (c + 1), :], alog_ref[...], tri_ref[...],
                    pan_scr, etot_scr, c, forward=True)

    def body(ci, carry):
        r0 = pl.multiple_of(ci * CHUNK, CHUNK)

        def emit(g, y):
            yf_ref[0, pl.ds(r0, CHUNK), pl.ds(GROUP_W * g, GROUP_W)] = y

        _ssd_chunk(act_ref, r0, pan_scr, etot_scr, ci, state_scr, gb_scr, emit, lane0=0,
                   forward=True)
        return carry

    lax.fori_loop(0, tm // CHUNK, body, 0, unroll=SSD_FWD_UNROLL)


def _ssd_bwd_kernel(act_ref, dt_ref, yf_ref, zs_ref, alog_ref, tri_ref, dskip_ref,
                    nw_ref, ys_ref, state_scr, pan_scr, etot_scr, gb_scr, *, tm):
    @pl.when(pl.program_id(1) == 0)
    def _():
        state_scr[...] = jnp.zeros_like(state_scr)

    nc = tm // CHUNK
    for c in range(nc):
        _ssd_panels(dt_ref[0, CHUNK * c : CHUNK * (c + 1), :], alog_ref[...], tri_ref[...],
                    pan_scr, etot_scr, c, forward=False)

    def body(cj, carry):
        ci = nc - 1 - cj
        r0 = pl.multiple_of(ci * CHUNK, CHUNK)
        rows = pl.ds(r0, CHUNK)

        def emit(g, y):
            gsl = pl.ds(GROUP_W * g, GROUP_W)
            xs = act_ref[0, rows, gsl].astype(F32)
            y = y + yf_ref[0, rows, gsl] + dskip_ref[:, gsl] * xs
            y = y * _silu(zs_ref[0, rows, gsl].astype(F32))
            ms = jnp.mean(y * y, axis=-1, keepdims=True)
            ys_ref[0, rows, gsl] = ((y * lax.rsqrt(ms + EPS)) * nw_ref[:, gsl]).astype(BF16)

        _ssd_chunk(act_ref, r0, pan_scr, etot_scr, ci, state_scr, gb_scr, emit, lane0=N_HEADS,
                   forward=False)
        return carry

    lax.fori_loop(0, nc, body, 0, unroll=SSD_BWD_UNROLL)


@functools.lru_cache(maxsize=None)
def _ssd_constants():
    tri_f = np.tril(np.ones((CHUNK, CHUNK)))
    tri_b = np.triu(np.ones((CHUNK, CHUNK)))
    return jnp.asarray(tri_f, BF16), jnp.asarray(tri_b, BF16)


def _ssd_fwd(act, dt, alog_p, tm):
    b, s, _ = act.shape
    nt = s // tm
    tri_f, _ = _ssd_constants()
    fixed = lambda bi, i: (0, 0)
    tile = lambda w: pl.BlockSpec((1, tm, w), lambda bi, i: (bi, i, 0))
    scratch = [
        pltpu.VMEM((N_BC_GROUPS, D_STATE, GROUP_W), F32),
        pltpu.VMEM((tm // CHUNK, N_PANELS, CHUNK, LANES), F32),
        pltpu.VMEM((tm // CHUNK, 8, LANES), F32),
        pltpu.VMEM((2, CHUNK, CHUNK), F32),
    ]

    return pl.pallas_call(
        functools.partial(_ssd_fwd_kernel, tm=tm),
        grid=(b, nt),
        in_specs=[
            tile(CONV_DIM),
            tile(LANES),
            pl.BlockSpec((1, LANES), fixed),
            pl.BlockSpec((CHUNK, CHUNK), fixed),
        ],
        out_specs=tile(D_SSM),
        out_shape=jax.ShapeDtypeStruct((b, s, D_SSM), F32),
        scratch_shapes=scratch,
        compiler_params=_cparams(2),
        name="ssd_fwd",
    )(act, dt, alog_p, tri_f)


def _epilogue_kernel(x_ref, yf_ref, ys_ref, p_ref, wo_ref, wg_ref, wp_ref, fw_ref, o_ref):
    mix = jnp.concatenate([yf_ref[...], ys_ref[...]], axis=1)
    h = x_ref[...] + jnp.dot(mix, wo_ref[...], preferred_element_type=F32)
    gate = _sigmoid(jnp.dot(h.astype(BF16), wg_ref[...], preferred_element_type=F32))
    ple = jnp.dot(p_ref[...].astype(BF16), wp_ref[...], preferred_element_type=F32)
    h = h + ple * gate
    ms = jnp.mean(h * h, axis=-1, keepdims=True)
    o_ref[...] = (h * lax.rsqrt(ms + EPS)) * fw_ref[...]


def _epilogue(x2, y_f, y_s, p2, w_out_b, w_gate_b, w_ple_b, final_w, tm):
    t = x2.shape[0]
    row = lambda i: (i, 0)
    fixed = lambda i: (0, 0)
    return pl.pallas_call(
        _epilogue_kernel,
        grid=(t // tm,),
        in_specs=[
            pl.BlockSpec((tm, D_MODEL), row),
            pl.BlockSpec((tm, 512), row),
            pl.BlockSpec((tm, D_SSM), row),
            pl.BlockSpec((tm, D_PLE), row),
            pl.BlockSpec((2048, D_MODEL), fixed),
            pl.BlockSpec((D_MODEL, D_MODEL), fixed),
            pl.BlockSpec((D_PLE, D_MODEL), fixed),
            pl.BlockSpec((1, D_MODEL), fixed),
        ],
        out_specs=pl.BlockSpec((tm, D_MODEL), row),
        out_shape=jax.ShapeDtypeStruct((t, D_MODEL), F32),
        compiler_params=_cparams(1),
        name="epilogue",
    )(x2, y_f, y_s, p2, w_out_b, w_gate_b, w_ple_b, final_w)


def _tail_kernel(act_ref, dt_ref, yf_ref, zs_ref, x_ref, yfn_ref, p_ref, alog_ref, tri_ref,
                 dskip_ref, nw_ref, wo_ref, wg_ref, wp_ref, fw_ref, o_ref, state_scr, pan_scr,
                 etot_scr, gb_scr, ys_scr, ysp_scr, *, tm, nt, ntot):
    q = pl.program_id(0)

    @pl.when(q == 0)
    def _():
        ys_scr[...] = jnp.zeros_like(ys_scr)

    ysp_scr[...] = ys_scr[...]

    @pl.when(jnp.minimum(q, ntot - 1) % nt == 0)
    def _():
        state_scr[...] = jnp.zeros_like(state_scr)

    nc = tm // CHUNK
    for c in range(nc):
        _ssd_panels(dt_ref[0, CHUNK * c : CHUNK * (c + 1), :], alog_ref[...], tri_ref[...],
                    pan_scr, etot_scr, c, forward=False)

    for ci in reversed(range(nc)):
        rows = slice(CHUNK * ci, CHUNK * (ci + 1))

        def emit(g, y, rows=rows):
            gsl = slice(GROUP_W * g, GROUP_W * (g + 1))
            xs = act_ref[0, rows, gsl].astype(F32)
            y = y + yf_ref[0, rows, gsl] + dskip_ref[:, gsl] * xs
            y = y * _silu(zs_ref[0, rows, gsl].astype(F32))
            ms = jnp.mean(y * y, axis=-1, keepdims=True)
            ys_scr[rows, gsl] = ((y * lax.rsqrt(ms + EPS)) * nw_ref[:, gsl]).astype(BF16)

        _ssd_chunk(act_ref, CHUNK * ci, pan_scr, etot_scr, ci, state_scr, gb_scr, emit,
                   lane0=N_HEADS, forward=False)

    mix = jnp.concatenate([yfn_ref[0], ysp_scr[...]], axis=1)
    h = x_ref[0] + jnp.dot(mix, wo_ref[...], preferred_element_type=F32)
    gate = _sigmoid(jnp.dot(h.astype(BF16), wg_ref[...], preferred_element_type=F32))
    ple = jnp.dot(p_ref[0].astype(BF16), wp_ref[...], preferred_element_type=F32)
    h = h + ple * gate
    ms = jnp.mean(h * h, axis=-1, keepdims=True)
    o_ref[0] = (h * lax.rsqrt(ms + EPS)) * fw_ref[...]


def _tail(act, dt, y_fwd, z_s, x, y_f, p, alog_p, dskip_x, ssd_nw, w_out_b, w_gate_b, w_ple_b,
          final_w, tm):
    b, s, _ = act.shape
    nt = s // tm
    ntot = b * nt
    _, tri_b = _ssd_constants()

    def scan_tile(q):
        qs = jnp.minimum(q, ntot - 1)
        return qs // nt, nt - 1 - qs % nt, 0

    def epi_tile(q):
        return scan_tile(jnp.maximum(q - 1, 0))

    stile = lambda w: pl.BlockSpec((1, tm, w), scan_tile)
    etile = lambda w: pl.BlockSpec((1, tm, w), epi_tile)
    fixed = lambda shape: pl.BlockSpec(shape, lambda q: (0, 0))
    return pl.pallas_call(
        functools.partial(_tail_kernel, tm=tm, nt=nt, ntot=ntot),
        grid=(ntot + 1,),
        in_specs=[
            stile(CONV_DIM), stile(LANES), stile(D_SSM), stile(D_SSM),
            etile(D_MODEL), etile(512), etile(D_PLE),
            fixed((1, LANES)), fixed((CHUNK, CHUNK)), fixed((1, D_SSM)), fixed((1, D_SSM)),
            fixed((2048, D_MODEL)), fixed((D_MODEL, D_MODEL)), fixed((D_PLE, D_MODEL)),
            fixed((1, D_MODEL)),
        ],
        out_specs=etile(D_MODEL),
        out_shape=jax.ShapeDtypeStruct((b, s, D_MODEL), F32),
        scratch_shapes=[
            pltpu.VMEM((N_BC_GROUPS, D_STATE, GROUP_W), F32),
            pltpu.VMEM((tm // CHUNK, N_PANELS, CHUNK, LANES), F32),
            pltpu.VMEM((tm // CHUNK, 8, LANES), F32),
            pltpu.VMEM((2, CHUNK, CHUNK), F32),
            pltpu.VMEM((tm, D_SSM), BF16),
            pltpu.VMEM((tm, D_SSM), BF16),
        ],
        compiler_params=_cparams(1),
        name="tail",
    )(act, dt, y_fwd, z_s, x, y_f, p, alog_p, tri_b, dskip_x, ssd_nw, w_out_b, w_gate_b,
      w_ple_b, final_w)


def _trunk(x, p, norm_w, w_in, w_fmix, conv_w, conv_b, a_log_f, a_log_b, dt_bias_f,
           dt_bias_b, d_skip, ssd_norm_w, w_out, w_ple_in, w_ple_gate, final_norm_w):
    b, s, _ = x.shape
    t = b * s
    tm = 512
    assert s % tm == 0, s

    pad = jnp.zeros((LANES - 2 * N_HEADS,), F32)
    bias_p = jnp.concatenate([dt_bias_f[0], dt_bias_b[0], pad])[None, :]
    alog_p = jnp.concatenate([a_log_f[0], a_log_b[0], pad])[None, :]
    dskip_x = jnp.repeat(d_skip[0], HEAD_DIM)[None, :]

    w_in_b = jnp.pad(w_in[0], ((0, 0), (0, D_IN_PAD - D_IN_PROJ))).astype(BF16)
    z_f, u_f, z_s, act, dt = _in_proj(
        x, norm_w[0][None, :], w_in_b, conv_w[0], conv_b[0][None, :], bias_p, tm
    )

    wbd = jax.scipy.linalg.block_diag(*[w_fmix[0, g] for g in range(N_FGROUPS)]).astype(BF16)
    y_f = _fnet(u_f, z_f, wbd)

    y_fwd = _ssd_fwd(act, dt, alog_p, 2 * tm)
    return _tail(
        act, dt, y_fwd, z_s, x, y_f, p[0], alog_p, dskip_x, ssd_norm_w[0][None, :],
        w_out[0].astype(BF16), w_ple_gate[0].astype(BF16), w_ple_in[0].astype(BF16),
        final_norm_w[None, :], tm,
    )


def kernel(x_prompt, x_sample, p_prompt, p_sample, norm_w, w_in, w_fmix, conv_w, conv_b, a_log_f, a_log_b, dt_bias_f, dt_bias_b, d_skip, ssd_norm_w, w_out, w_ple_in, w_ple_gate, final_norm_w):
    weights = (norm_w, w_in, w_fmix, conv_w, conv_b, a_log_f, a_log_b, dt_bias_f, dt_bias_b,
               d_skip, ssd_norm_w, w_out, w_ple_in, w_ple_gate, final_norm_w)
    return (_trunk(x_prompt, p_prompt, *weights), _trunk(x_sample, p_sample, *weights))
```

```python
import functools
import math

import jax
import jax.numpy as jnp
import numpy as np
from jax import lax
from jax.experimental import pallas as pl
from jax.experimental.pallas import tpu as pltpu

F32 = jnp.float32
BF16 = jnp.bfloat16

D_MODEL = 1024
D_FOURIER = 512
N_FGROUPS = 8
F_GROUP = 64
D_SSM = 1536
HEAD_DIM = 64
N_HEADS = 24
N_BC_GROUPS = 4
HEADS_PER_GROUP = 6
GROUP_W = HEADS_PER_GROUP * HEAD_DIM
D_STATE = 128
D_CONV = 5
CONV_DIM = 2560
CHUNK = 128
D_PLE = 256
D_IN_PROJ = 5168
D_IN_PAD = 5248
DT_COL = 5120
EPS = 1e-6

LANES = 128
HALO = 8
SSD_FWD_UNROLL = 8
FFT_N2 = 16
FFT_K1_TILE = 16
FFT_K1_STEP = 64
FFT_SLABS = 2
VMEM_LIMIT = 56 * 1024 * 1024


def _cparams(n_axes):
    return pltpu.CompilerParams(
        dimension_semantics=("arbitrary",) * n_axes, vmem_limit_bytes=VMEM_LIMIT
    )


def _sigmoid(v):
    return 0.5 + 0.5 * jnp.tanh(0.5 * v)


def _silu(v):
    hv = 0.5 * v
    return hv + hv * jnp.tanh(hv)


def _in_proj_kernel(xp_ref, x_ref, xn_ref, nw_ref, w_ref, cw_ref, cb_ref, dtb_ref,
                    zf_ref, uf_ref, zs_ref, act_ref, dt_ref, xe_scr, co_scr, *, tm):
    i = pl.program_id(1)
    nt = pl.num_programs(1)

    def norm(x):
        ms = jnp.mean(x * x, axis=-1, keepdims=True)
        return (x * lax.rsqrt(ms + EPS)) * nw_ref[...]

    u32 = norm(x_ref[0])
    up = jnp.where(i == 0, 0.0, norm(xp_ref[0]))
    un = jnp.where(i == nt - 1, 0.0, norm(xn_ref[0]))
    u = u32.astype(BF16)
    u_ext = jnp.concatenate([up, u32, un], axis=0).astype(BF16)

    def seg(lhs, c0, width):
        return jnp.dot(lhs, w_ref[:, c0 : c0 + width], preferred_element_type=F32)

    mid = D_CONV // 2

    def conv_block(j):
        xe = seg(u_ext, 2560 + 512 * j, 512)
        for c in range(512 // LANES):
            xe_scr[c] = xe[:, LANES * c : LANES * (c + 1)]
        for c in range(512 // LANES):
            cols = slice(512 * j + LANES * c, 512 * j + LANES * (c + 1))
            taps = [
                xe_scr[c, pl.ds(HALO - mid + i, tm // 2, stride=2), :] for i in range(D_CONV + 1)
            ]
            for par in range(2):
                acc = cb_ref[:, cols]
                for k in range(D_CONV):
                    acc = acc + taps[k + par] * cw_ref[k : k + 1, cols]
                co_scr[c, pl.ds(par, tm // 2, stride=2), :] = _silu(acc)
        for c in range(512 // LANES):
            cols = slice(512 * j + LANES * c, 512 * j + LANES * (c + 1))
            act_ref[0, :, cols] = co_scr[c].astype(BF16)

    def z_block(j):
        v = seg(u, 512 * j, 512)
        if j == 0:
            zf_ref[0] = v.astype(BF16)
        elif j == 1:
            for c in range(512 // LANES):
                uf_ref[0, c] = v[:, LANES * c : LANES * (c + 1)]
        else:
            zs_ref[0, :, 512 * (j - 2) : 512 * (j - 1)] = _silu(v).astype(BF16)

    for j in range(CONV_DIM // 512):
        conv_block(j)
        z_block(j)
    dt_pre = seg(u, DT_COL, LANES) + dtb_ref[...]
    dt_ref[0] = jnp.maximum(dt_pre, 0.0) + jnp.log1p(jnp.exp(-jnp.abs(dt_pre)))


def _in_proj(x, norm_w, w_in_b, conv_w, conv_b, dt_bias_p, tm):
    b, s, _ = x.shape
    tile = lambda w: pl.BlockSpec((1, tm, w), lambda bi, i: (bi, i, 0))
    fixed = lambda bi, i: (0, 0)
    outs = [(512, BF16), None, (D_SSM, BF16), (CONV_DIM, BF16), (LANES, F32)]
    out_specs = [tile(o[0]) if o else None for o in outs]
    out_shape = [jax.ShapeDtypeStruct((b, s, o[0]), o[1]) if o else None for o in outs]
    out_specs[1] = pl.BlockSpec((1, 512 // LANES, tm, LANES), lambda bi, i: (bi, 0, i, 0))
    out_shape[1] = jax.ShapeDtypeStruct((b, 512 // LANES, s, LANES), F32)
    return pl.pallas_call(
        functools.partial(_in_proj_kernel, tm=tm),
        grid=(b, s // tm),
        in_specs=[
            pl.BlockSpec(
                (1, HALO, D_MODEL),
                lambda bi, i: (bi, jnp.maximum(i * (tm // HALO) - 1, 0), 0),
            ),
            tile(D_MODEL),
            pl.BlockSpec(
                (1, HALO, D_MODEL),
                lambda bi, i: (bi, jnp.minimum((i + 1) * (tm // HALO), s // HALO - 1), 0),
            ),
            pl.BlockSpec((1, D_MODEL), fixed),
            pl.BlockSpec((D_MODEL, D_IN_PAD), fixed),
            pl.BlockSpec((D_CONV, CONV_DIM), fixed),
            pl.BlockSpec((1, CONV_DIM), fixed),
            pl.BlockSpec((1, LANES), fixed),
        ],
        out_specs=out_specs,
        out_shape=out_shape,
        scratch_shapes=[
            pltpu.VMEM((512 // LANES, tm + 2 * HALO, LANES), F32),
            pltpu.VMEM((512 // LANES, tm, LANES), F32),
        ],
        compiler_params=_cparams(2),
        name="in_proj",
    )(x, x, x, norm_w, w_in_b, conv_w, conv_b, dt_bias_p)


def _fft1_kernel(u_ref, w1_ref, tc_ref, ts_ref, re_ref, mi_ref):
    n1 = re_ref.shape[2]
    w1 = w1_ref[...]
    for s2 in range(FFT_N2):
        x = jnp.concatenate(
            [u_ref[0, c, pl.ds(s2, n1, stride=FFT_N2), :] for c in range(FFT_SLABS)], axis=1
        )
        pq = jnp.dot(w1, x.astype(BF16), preferred_element_type=F32)
        p = pq[:n1]
        qm = pq[n1:]
        c = jnp.concatenate([tc_ref[s2]] * FFT_SLABS, axis=1)
        s = jnp.concatenate([ts_ref[s2]] * FFT_SLABS, axis=1)
        re_ref[0, s2] = (p * c - qm * s).astype(BF16)
        mi_ref[0, s2] = (p * s + qm * c).astype(BF16)


def _fft3_kernel(re_ref, mi_ref, k3_ref, cs_ref, wbd_ref, zf_ref, y_ref, mix_scr, *, scale):
    @pl.when((pl.program_id(0) == 0) & (pl.program_id(1) == 0))
    def _():
        mix = jnp.dot(cs_ref[...], wbd_ref[...], preferred_element_type=F32) * scale
        mix_scr[...] = mix.astype(BF16)

    kt = FFT_K1_TILE
    rows = FFT_N2 * kt
    zs = []
    for t in range(FFT_K1_STEP // kt):
        sl = slice(kt * t, kt * (t + 1))
        rhs = jnp.concatenate(
            [re_ref[0, :, sl, :].reshape(rows, 512), mi_ref[0, :, sl, :].reshape(rows, 512)],
            axis=0,
        )
        z = jnp.dot(k3_ref[...], rhs, preferred_element_type=F32)
        zs.append(jnp.concatenate([z[:rows], z[rows:]], axis=1).astype(BF16))
    y = jnp.dot(jnp.concatenate(zs, axis=0), mix_scr[...], preferred_element_type=F32)
    for t in range(FFT_K1_STEP // kt):
        sl = slice(kt * t, kt * (t + 1))
        zf = zf_ref[0, :, sl, :].reshape(rows, 512).astype(F32)
        yt = y[rows * t : rows * (t + 1)] * _silu(zf)
        y_ref[0, :, sl, :] = yt.astype(BF16).reshape(FFT_N2, kt, 512)


def _dft_cos_sin(n):
    j = np.arange(n, dtype=np.int64)
    ang = 2.0 * np.pi * ((j[:, None] * j[None, :]) % n).astype(np.float64) / n
    return np.cos(ang), np.sin(ang)


@functools.lru_cache(maxsize=None)
def _fft_constants(s):
    n2 = FFT_N2
    n1 = s // n2
    c1, s1 = _dft_cos_sin(n1)
    w1 = np.concatenate([c1, s1], axis=0)
    k1 = np.arange(n1, dtype=np.int64)
    s2 = np.arange(n2, dtype=np.int64)
    ang = 2.0 * np.pi * ((s2[:, None] * k1[None, :]) % s).astype(np.float64) / s
    tc = np.repeat(np.cos(ang)[:, :, None], LANES, axis=2)
    ts = np.repeat(np.sin(ang)[:, :, None], LANES, axis=2)
    c2, s2m = _dft_cos_sin(n2)
    eye = np.eye(FFT_K1_TILE)
    kc = np.kron(c2, eye)
    ks = np.kron(s2m, eye)
    k3 = np.block([[kc, -ks], [ks, kc]])
    cc, sc = _dft_cos_sin(F_GROUP)
    eye_g = np.eye(N_FGROUPS)
    cs = np.concatenate([np.kron(eye_g, cc), -np.kron(eye_g, sc)], axis=0)
    return (
        jnp.asarray(w1, BF16),
        jnp.asarray(tc, F32),
        jnp.asarray(ts, F32),
        jnp.asarray(k3, BF16),
        jnp.asarray(cs, BF16),
    )


def _fnet(u_f4, z_f, wbd_b):
    b, _, s, _ = u_f4.shape
    n2 = FFT_N2
    n1 = s // n2
    assert s == n1 * n2 and n1 % FFT_K1_STEP == 0, s
    w1, tc, ts, k3, cs = _fft_constants(s)
    a_shape = jax.ShapeDtypeStruct((b, n2, n1, 512), BF16)
    re, mi = pl.pallas_call(
        _fft1_kernel,
        grid=(b, 512 // (FFT_SLABS * LANES)),
        in_specs=[
            pl.BlockSpec((1, FFT_SLABS, s, LANES), lambda i, j: (i, j, 0, 0)),
            pl.BlockSpec((2 * n1, n1), lambda i, j: (0, 0)),
            pl.BlockSpec((n2, n1, LANES), lambda i, j: (0, 0, 0)),
            pl.BlockSpec((n2, n1, LANES), lambda i, j: (0, 0, 0)),
        ],
        out_specs=[
            pl.BlockSpec((1, n2, n1, FFT_SLABS * LANES), lambda i, j: (i, 0, 0, j)),
            pl.BlockSpec((1, n2, n1, FFT_SLABS * LANES), lambda i, j: (i, 0, 0, j)),
        ],
        out_shape=[a_shape, a_shape],
        compiler_params=_cparams(2),
        name="fft1",
    )(u_f4, w1, tc, ts)

    blk = pl.BlockSpec((1, n2, FFT_K1_STEP, 512), lambda i, j: (i, 0, j, 0))
    fixed = lambda i, j: (0, 0)
    scale = 1.0 / math.sqrt(float(s) * F_GROUP)
    y4 = pl.pallas_call(
        functools.partial(_fft3_kernel, scale=scale),
        grid=(b, n1 // FFT_K1_STEP),
        in_specs=[
            blk,
            blk,
            pl.BlockSpec(k3.shape, fixed),
            pl.BlockSpec(cs.shape, fixed),
            pl.BlockSpec((512, 512), fixed),
            blk,
        ],
        out_specs=blk,
        out_shape=jax.ShapeDtypeStruct((b, n2, n1, 512), BF16),
        scratch_shapes=[pltpu.VMEM(cs.shape, BF16)],
        compiler_params=_cparams(2),
        name="fft3",
    )(re, mi, k3, cs, wbd_b, z_f.reshape(b, n2, n1, 512))
    return y4.reshape(b, s, 512)


def _split3_dot(tri, v):
    v1 = v.astype(BF16)
    r1 = v - v1.astype(F32)
    v2 = r1.astype(BF16)
    v3 = (r1 - v2.astype(F32)).astype(BF16)
    d = lambda w: jnp.dot(tri, w, preferred_element_type=F32)
    return d(v1) + d(v2) + d(v3)


N_PANELS = 4
LOG2E = math.log2(math.e)


def _ssd_panels(dt, alog, tri, pan_ref, etot_ref, c, *, forward):
    a = dt * (-jnp.exp(alog))
    acs = _split3_dot(tri, a)
    total = acs[CHUNK - 1 : CHUNK, :] if forward else acs[0:1, :]
    acs2 = acs * LOG2E
    l2dt = jnp.log2(dt)
    pan_ref[c, 0] = acs2
    pan_ref[c, 1] = (acs2 - l2dt).T
    pan_ref[c, 2] = l2dt.T
    pan_ref[c, 3] = (dt * jnp.exp(total - acs)).T
    etot_ref[c] = jnp.broadcast_to(jnp.exp(total), (8, LANES))


def _ssd_chunk(act_ref, r0, pan_ref, etot_ref, ci, state_ref, gb_ref, emit, *, lane0, forward):
    rows = pl.ds(r0, CHUNK)
    e_tot = etot_ref[ci][0:1, :]

    li = lax.broadcasted_iota(jnp.int32, (CHUNK, CHUNK), 0)
    si = lax.broadcasted_iota(jnp.int32, (CHUNK, CHUNK), 1)
    keep = (li >= si) if forward else (li <= si)
    left = lax.broadcasted_iota(jnp.int32, (CHUNK, LANES), 1) < HEAD_DIM
    left_row = lax.broadcasted_iota(jnp.int32, (1, LANES), 1) < HEAD_DIM
    nt_dims = (((1,), (1,)), ((), ()))

    for g in range(N_BC_GROUPS):
        bg = act_ref[0, rows, pl.ds(D_SSM + D_STATE * g, D_STATE)]
        cg = act_ref[0, rows, pl.ds(D_SSM + 512 + D_STATE * g, D_STATE)]
        cb = lax.dot_general(cg, bg, nt_dims, preferred_element_type=F32)
        gb_ref[0] = jnp.where(keep, cb, 0.0).astype(BF16)
        gb_ref[1] = bg.astype(F32).T.astype(BF16)
        y_off = jnp.dot(cg, state_ref[g].astype(BF16), preferred_element_type=F32)
        ys, sts = [], []
        for j in range(HEADS_PER_GROUP // 2):
            pair = (HEADS_PER_GROUP // 2) * g + j
            psl = slice(LANES * j, LANES * (j + 1))
            xp = act_ref[0, rows, pl.ds(LANES * pair, LANES)]
            zero = jnp.zeros_like(xp)
            xbd = jnp.concatenate([jnp.where(left, xp, zero), jnp.where(left, zero, xp)], axis=0)
            gl, wl, es, et = [], [], [], []
            for h in (2 * pair, 2 * pair + 1):
                hl = lane0 + h
                row = lambda p: pan_ref[ci, p, pl.ds(hl, 1), :]
                colb = jnp.broadcast_to(pan_ref[ci, 0][:, hl : hl + 1], (CHUNK, LANES))
                dec = jnp.exp2(jnp.minimum(colb - row(1), row(2)))
                gl.append(dec.astype(BF16) * gb_ref[0])
                wl.append(gb_ref[1] * row(3).astype(BF16))
                es.append(colb)
                et.append(jnp.broadcast_to(e_tot[:, hl : hl + 1], (1, LANES)))
            lhs = jnp.concatenate(
                [jnp.concatenate(gl, axis=1), jnp.concatenate(wl, axis=1)], axis=0
            )
            res = jnp.dot(lhs, xbd, preferred_element_type=F32)
            ys.append(res[:CHUNK] + y_off[:, psl] * jnp.exp2(jnp.where(left, es[0], es[1])))
            sts.append(
                state_ref[g, :, psl] * jnp.where(left_row, et[0], et[1]) + res[CHUNK:]
            )
        state_ref[g] = jnp.concatenate(sts, axis=1)
        emit(g, jnp.concatenate(ys, axis=1))


def _ssd_fwd_kernel(act_ref, dt_ref, alog_ref, tri_ref, yf_ref, state_scr, pan_scr, etot_scr,
                    gb_scr, *, tm):
    @pl.when(pl.program_id(1) == 0)
    def _():
        state_scr[...] = jnp.zeros_like(state_scr)

    for c in range(tm // CHUNK):
        _ssd_panels(dt_ref[0, CHUNK * c : CHUNK * (c + 1), :], alog_ref[...], tri_ref[...],
                    pan_scr, etot_scr, c, forward=True)

    def body(ci, carry):
        r0 = pl.multiple_of(ci * CHUNK, CHUNK)

        def emit(g, y):
            yf_ref[0, pl.ds(r0, CHUNK), pl.ds(GROUP_W * g, GROUP_W)] = y

        _ssd_chunk(act_ref, r0, pan_scr, etot_scr, ci, state_scr, gb_scr, emit, lane0=0,
                   forward=True)
        return carry

    lax.fori_loop(0, tm // CHUNK, body, 0, unroll=SSD_FWD_UNROLL)


@functools.lru_cache(maxsize=None)
def _ssd_constants():
    tri_f = np.tril(np.ones((CHUNK, CHUNK)))
    tri_b = np.triu(np.ones((CHUNK, CHUNK)))
    return jnp.asarray(tri_f, BF16), jnp.asarray(tri_b, BF16)


def _ssd_fwd(act, dt, alog_p, tm):
    b, s, _ = act.shape
    nt = s // tm
    tri_f, _ = _ssd_constants()
    fixed = lambda bi, i: (0, 0)
    tile = lambda w: pl.BlockSpec((1, tm, w), lambda bi, i: (bi, i, 0))
    scratch = [
        pltpu.VMEM((N_BC_GROUPS, D_STATE, GROUP_W), F32),
        pltpu.VMEM((tm // CHUNK, N_PANELS, CHUNK, LANES), F32),
        pltpu.VMEM((tm // CHUNK, 8, LANES), F32),
        pltpu.VMEM((2, CHUNK, CHUNK), BF16),
    ]

    return pl.pallas_call(
        functools.partial(_ssd_fwd_kernel, tm=tm),
        grid=(b, nt),
        in_specs=[
            tile(CONV_DIM),
            tile(LANES),
            pl.BlockSpec((1, LANES), fixed),
            pl.BlockSpec((CHUNK, CHUNK), fixed),
        ],
        out_specs=tile(D_SSM),
        out_shape=jax.ShapeDtypeStruct((b, s, D_SSM), F32),
        scratch_shapes=scratch,
        compiler_params=_cparams(2),
        name="ssd_fwd",
    )(act, dt, alog_p, tri_f)


def _tail_kernel(act_ref, dt_ref, yf_ref, zs_ref, x_ref, yfn_ref, p_ref, alog_ref, tri_ref,
                 dskip_ref, nw_ref, wo_ref, wg_ref, wp_ref, fw_ref, o_ref, state_scr, pan_scr,
                 etot_scr, gb_scr, ys_scr, ysp_scr, *, tm, nt, ntot):
    q = pl.program_id(0)

    @pl.when(q == 0)
    def _():
        ys_scr[...] = jnp.zeros_like(ys_scr)

    ysp_scr[...] = ys_scr[...]

    @pl.when(jnp.minimum(q, ntot - 1) % nt == 0)
    def _():
        state_scr[...] = jnp.zeros_like(state_scr)

    nc = tm // CHUNK
    for c in range(nc):
        _ssd_panels(dt_ref[0, CHUNK * c : CHUNK * (c + 1), :], alog_ref[...], tri_ref[...],
                    pan_scr, etot_scr, c, forward=False)

    for ci in reversed(range(nc)):
        rows = slice(CHUNK * ci, CHUNK * (ci + 1))

        def emit(g, y, rows=rows):
            gsl = slice(GROUP_W * g, GROUP_W * (g + 1))
            xs = act_ref[0, rows, gsl].astype(F32)
            y = y + yf_ref[0, rows, gsl] + dskip_ref[:, gsl] * xs
            y = y * zs_ref[0, rows, gsl].astype(F32)
            ms = jnp.mean(y * y, axis=-1, keepdims=True)
            ys_scr[rows, gsl] = ((y * lax.rsqrt(ms + EPS)) * nw_ref[:, gsl]).astype(BF16)

        _ssd_chunk(act_ref, CHUNK * ci, pan_scr, etot_scr, ci, state_scr, gb_scr, emit,
                   lane0=N_HEADS, forward=False)

    mix = jnp.concatenate([yfn_ref[0], ysp_scr[...]], axis=1)
    h = x_ref[0] + jnp.dot(mix, wo_ref[...], preferred_element_type=F32)
    gate = _sigmoid(jnp.dot(h.astype(BF16), wg_ref[...], preferred_element_type=F32))
    ple = jnp.dot(p_ref[0].astype(BF16), wp_ref[...], preferred_element_type=F32)
    h = h + ple * gate
    ms = jnp.mean(h * h, axis=-1, keepdims=True)
    o_ref[0] = (h * lax.rsqrt(ms + EPS)) * fw_ref[...]


def _tail(act, dt, y_fwd, z_s, x, y_f, p, alog_p, dskip_x, ssd_nw, w_out_b, w_gate_b, w_ple_b,
          final_w, tm):
    b, s, _ = act.shape
    nt = s // tm
    ntot = b * nt
    _, tri_b = _ssd_constants()

    def scan_tile(q):
        qs = jnp.minimum(q, ntot - 1)
        return qs // nt, nt - 1 - qs % nt, 0

    def epi_tile(q):
        return scan_tile(jnp.maximum(q - 1, 0))

    stile = lambda w: pl.BlockSpec((1, tm, w), scan_tile)
    etile = lambda w: pl.BlockSpec((1, tm, w), epi_tile)
    fixed = lambda shape: pl.BlockSpec(shape, lambda q: (0, 0))
    return pl.pallas_call(
        functools.partial(_tail_kernel, tm=tm, nt=nt, ntot=ntot),
        grid=(ntot + 1,),
        in_specs=[
            stile(CONV_DIM), stile(LANES), stile(D_SSM), stile(D_SSM),
            etile(D_MODEL), etile(512), etile(D_PLE),
            fixed((1, LANES)), fixed((CHUNK, CHUNK)), fixed((1, D_SSM)), fixed((1, D_SSM)),
            fixed((2048, D_MODEL)), fixed((D_MODEL, D_MODEL)), fixed((D_PLE, D_MODEL)),
            fixed((1, D_MODEL)),
        ],
        out_specs=etile(D_MODEL),
        out_shape=jax.ShapeDtypeStruct((b, s, D_MODEL), F32),
        scratch_shapes=[
            pltpu.VMEM((N_BC_GROUPS, D_STATE, GROUP_W), F32),
            pltpu.VMEM((tm // CHUNK, N_PANELS, CHUNK, LANES), F32),
            pltpu.VMEM((tm // CHUNK, 8, LANES), F32),
            pltpu.VMEM((2, CHUNK, CHUNK), BF16),
            pltpu.VMEM((tm, D_SSM), BF16),
            pltpu.VMEM((tm, D_SSM), BF16),
        ],
        compiler_params=_cparams(1),
        name="tail",
    )(act, dt, y_fwd, z_s, x, y_f, p, alog_p, tri_b, dskip_x, ssd_nw, w_out_b, w_gate_b,
      w_ple_b, final_w)


def _trunk(x, p, norm_w, w_in, w_fmix, conv_w, conv_b, a_log_f, a_log_b, dt_bias_f,
           dt_bias_b, d_skip, ssd_norm_w, w_out, w_ple_in, w_ple_gate, final_norm_w):
    b, s, _ = x.shape
    t = b * s
    tm = 512
    assert s % tm == 0, s

    pad = jnp.zeros((LANES - 2 * N_HEADS,), F32)
    bias_p = jnp.concatenate([dt_bias_f[0], dt_bias_b[0], pad])[None, :]
    alog_p = jnp.concatenate([a_log_f[0], a_log_b[0], pad])[None, :]
    dskip_x = jnp.repeat(d_skip[0], HEAD_DIM)[None, :]

    w_in_b = jnp.pad(w_in[0].astype(BF16), ((0, 0), (0, D_IN_PAD - D_IN_PROJ)))
    z_f, u_f, z_s, act, dt = _in_proj(
        x, norm_w[0][None, :], w_in_b, conv_w[0], conv_b[0][None, :], bias_p, tm
    )

    wbd = jax.scipy.linalg.block_diag(*[w_fmix[0, g] for g in range(N_FGROUPS)]).astype(BF16)
    y_f = _fnet(u_f, z_f, wbd)

    y_fwd = _ssd_fwd(act, dt, alog_p, 2 * tm)
    return _tail(
        act, dt, y_fwd, z_s, x, y_f, p[0], alog_p, dskip_x, ssd_norm_w[0][None, :],
        w_out[0].astype(BF16), w_ple_gate[0].astype(BF16), w_ple_in[0].astype(BF16),
        final_norm_w[None, :], tm,
    )


def kernel(x_prompt, x_sample, p_prompt, p_sample, norm_w, w_in, w_fmix, conv_w, conv_b, a_log_f, a_log_b, dt_bias_f, dt_bias_b, d_skip, ssd_norm_w, w_out, w_ple_in, w_ple_gate, final_norm_w):
    weights = (norm_w, w_in, w_fmix, conv_w, conv_b, a_log_f, a_log_b, dt_bias_f, dt_bias_b,
               d_skip, ssd_norm_w, w_out, w_ple_in, w_ple_gate, final_norm_w)
    return (_trunk(x_prompt, p_prompt, *weights), _trunk(x_sample, p_sample, *weights))
```

```python
import functools
import math

import jax
import jax.numpy as jnp
import numpy as np
from jax import lax
from jax.experimental import pallas as pl
from jax.experimental.pallas import tpu as pltpu

F32 = jnp.float32
BF16 = jnp.bfloat16

D_MODEL = 1024
D_FOURIER = 512
N_FGROUPS = 8
F_GROUP = 64
D_SSM = 1536
HEAD_DIM = 64
N_HEADS = 24
N_BC_GROUPS = 4
HEADS_PER_GROUP = 6
GROUP_W = HEADS_PER_GROUP * HEAD_DIM
D_STATE = 128
D_CONV = 5
CONV_DIM = 2560
CHUNK = 128
D_PLE = 256
D_IN_PROJ = 5168
D_IN_PAD = 5248
DT_COL = 5120
EPS = 1e-6

LANES = 128
HALO = 8
SSD_FWD_UNROLL = 8
FFT_N2 = 16
FFT_K1_TILE = 16
FFT_K1_STEP = 128
FFT_SLABS = 2
VMEM_LIMIT = 56 * 1024 * 1024


def _cparams(n_axes):
    return pltpu.CompilerParams(
        dimension_semantics=("arbitrary",) * n_axes, vmem_limit_bytes=VMEM_LIMIT
    )


def _sigmoid(v):
    return 0.5 + 0.5 * jnp.tanh(0.5 * v)


def _silu(v):
    hv = 0.5 * v
    return hv + hv * jnp.tanh(hv)


def _in_proj_kernel(xp_ref, x_ref, xn_ref, nw_ref, w_ref, cw_ref, cb_ref, dtb_ref,
                    zf_ref, uf_ref, zs_ref, act_ref, dt_ref, xe_scr, co_scr, *, tm):
    i = pl.program_id(1)
    nt = pl.num_programs(1)

    def norm(x):
        ms = jnp.mean(x * x, axis=-1, keepdims=True)
        return (x * lax.rsqrt(ms + EPS)) * nw_ref[...]

    u32 = norm(x_ref[0])
    up = jnp.where(i == 0, 0.0, norm(xp_ref[0]))
    un = jnp.where(i == nt - 1, 0.0, norm(xn_ref[0]))
    u = u32.astype(BF16)
    u_ext = jnp.concatenate([up, u32, un], axis=0).astype(BF16)

    def seg(lhs, c0, width):
        return jnp.dot(lhs, w_ref[:, c0 : c0 + width], preferred_element_type=F32)

    mid = D_CONV // 2

    def conv_block(j):
        xe = seg(u_ext, 2560 + 512 * j, 512)
        for c in range(512 // LANES):
            xe_scr[c] = xe[:, LANES * c : LANES * (c + 1)]
        for c in range(512 // LANES):
            cols = slice(512 * j + LANES * c, 512 * j + LANES * (c + 1))
            taps = [
                xe_scr[c, pl.ds(HALO - mid + i, tm // 2, stride=2), :] for i in range(D_CONV + 1)
            ]
            for par in range(2):
                acc = cb_ref[:, cols]
                for k in range(D_CONV):
                    acc = acc + taps[k + par] * cw_ref[k : k + 1, cols]
                co_scr[c, pl.ds(par, tm // 2, stride=2), :] = _silu(acc)
        for c in range(512 // LANES):
            cols = slice(512 * j + LANES * c, 512 * j + LANES * (c + 1))
            act_ref[0, :, cols] = co_scr[c].astype(BF16)

    def z_block(j):
        v = seg(u, 512 * j, 512)
        if j == 0:
            zf_ref[0] = v.astype(BF16)
        elif j == 1:
            for c in range(512 // LANES):
                uf_ref[0, c] = v[:, LANES * c : LANES * (c + 1)]
        else:
            zs_ref[0, :, 512 * (j - 2) : 512 * (j - 1)] = _silu(v).astype(BF16)

    for j in range(CONV_DIM // 512):
        conv_block(j)
        z_block(j)
    dt_pre = seg(u, DT_COL, LANES) + dtb_ref[...]
    dt_ref[0] = jnp.maximum(dt_pre, 0.0) + jnp.log1p(jnp.exp(-jnp.abs(dt_pre)))


def _in_proj(x, norm_w, w_in_b, conv_w, conv_b, dt_bias_p, tm):
    b, s, _ = x.shape
    tile = lambda w: pl.BlockSpec((1, tm, w), lambda bi, i: (bi, i, 0))
    fixed = lambda bi, i: (0, 0)
    outs = [(512, BF16), None, (D_SSM, BF16), (CONV_DIM, BF16), (LANES, F32)]
    out_specs = [tile(o[0]) if o else None for o in outs]
    out_shape = [jax.ShapeDtypeStruct((b, s, o[0]), o[1]) if o else None for o in outs]
    out_specs[1] = pl.BlockSpec((1, 512 // LANES, tm, LANES), lambda bi, i: (bi, 0, i, 0))
    out_shape[1] = jax.ShapeDtypeStruct((b, 512 // LANES, s, LANES), F32)
    return pl.pallas_call(
        functools.partial(_in_proj_kernel, tm=tm),
        grid=(b, s // tm),
        in_specs=[
            pl.BlockSpec(
                (1, HALO, D_MODEL),
                lambda bi, i: (bi, jnp.maximum(i * (tm // HALO) - 1, 0), 0),
            ),
            tile(D_MODEL),
            pl.BlockSpec(
                (1, HALO, D_MODEL),
                lambda bi, i: (bi, jnp.minimum((i + 1) * (tm // HALO), s // HALO - 1), 0),
            ),
            pl.BlockSpec((1, D_MODEL), fixed),
            pl.BlockSpec((D_MODEL, D_IN_PAD), fixed, pipeline_mode=pl.Buffered(1)),
            pl.BlockSpec((D_CONV, CONV_DIM), fixed),
            pl.BlockSpec((1, CONV_DIM), fixed),
            pl.BlockSpec((1, LANES), fixed),
        ],
        out_specs=out_specs,
        out_shape=out_shape,
        scratch_shapes=[
            pltpu.VMEM((512 // LANES, tm + 2 * HALO, LANES), F32),
            pltpu.VMEM((512 // LANES, tm, LANES), F32),
        ],
        compiler_params=_cparams(2),
        name="in_proj",
    )(x, x, x, norm_w, w_in_b, conv_w, conv_b, dt_bias_p)


def _fft1_kernel(u_ref, w1_ref, tc_ref, ts_ref, re_ref, mi_ref):
    n1 = re_ref.shape[2]
    w1 = w1_ref[...]
    for s2 in range(FFT_N2):
        x = jnp.concatenate(
            [u_ref[0, c, pl.ds(s2, n1, stride=FFT_N2), :] for c in range(FFT_SLABS)], axis=1
        )
        pq = jnp.dot(w1, x.astype(BF16), preferred_element_type=F32)
        p = pq[:n1]
        qm = pq[n1:]
        c = jnp.concatenate([tc_ref[s2]] * FFT_SLABS, axis=1)
        s = jnp.concatenate([ts_ref[s2]] * FFT_SLABS, axis=1)
        re_ref[0, s2] = (p * c - qm * s).astype(BF16)
        mi_ref[0, s2] = (p * s + qm * c).astype(BF16)


def _fft3_kernel(re_ref, mi_ref, k3_ref, cs_ref, wbd_ref, zf_ref, y_ref, mix_scr, *, scale):
    @pl.when((pl.program_id(0) == 0) & (pl.program_id(1) == 0))
    def _():
        mix = jnp.dot(cs_ref[...], wbd_ref[...], preferred_element_type=F32) * scale
        mix_scr[...] = mix.astype(BF16)

    kt = FFT_K1_TILE
    rows = FFT_N2 * kt
    zs = []
    for t in range(FFT_K1_STEP // kt):
        sl = slice(kt * t, kt * (t + 1))
        rhs = jnp.concatenate(
            [re_ref[0, :, sl, :].reshape(rows, 512), mi_ref[0, :, sl, :].reshape(rows, 512)],
            axis=0,
        )
        z = jnp.dot(k3_ref[...], rhs, preferred_element_type=F32)
        zs.append(jnp.concatenate([z[:rows], z[rows:]], axis=1).astype(BF16))
    y = jnp.dot(jnp.concatenate(zs, axis=0), mix_scr[...], preferred_element_type=F32)
    for t in range(FFT_K1_STEP // kt):
        sl = slice(kt * t, kt * (t + 1))
        zf = zf_ref[0, :, sl, :].reshape(rows, 512).astype(F32)
        yt = y[rows * t : rows * (t + 1)] * _silu(zf)
        y_ref[0, :, sl, :] = yt.astype(BF16).reshape(FFT_N2, kt, 512)


def _dft_cos_sin(n):
    j = np.arange(n, dtype=np.int64)
    ang = 2.0 * np.pi * ((j[:, None] * j[None, :]) % n).astype(np.float64) / n
    return np.cos(ang), np.sin(ang)


@functools.lru_cache(maxsize=None)
def _fft_constants(s):
    n2 = FFT_N2
    n1 = s // n2
    c1, s1 = _dft_cos_sin(n1)
    w1 = np.concatenate([c1, s1], axis=0)
    k1 = np.arange(n1, dtype=np.int64)
    s2 = np.arange(n2, dtype=np.int64)
    ang = 2.0 * np.pi * ((s2[:, None] * k1[None, :]) % s).astype(np.float64) / s
    tc = np.repeat(np.cos(ang)[:, :, None], LANES, axis=2)
    ts = np.repeat(np.sin(ang)[:, :, None], LANES, axis=2)
    c2, s2m = _dft_cos_sin(n2)
    eye = np.eye(FFT_K1_TILE)
    kc = np.kron(c2, eye)
    ks = np.kron(s2m, eye)
    k3 = np.block([[kc, -ks], [ks, kc]])
    cc, sc = _dft_cos_sin(F_GROUP)
    eye_g = np.eye(N_FGROUPS)
    cs = np.concatenate([np.kron(eye_g, cc), -np.kron(eye_g, sc)], axis=0)
    return (
        jnp.asarray(w1, BF16),
        jnp.asarray(tc, F32),
        jnp.asarray(ts, F32),
        jnp.asarray(k3, BF16),
        jnp.asarray(cs, BF16),
    )


def _fnet(u_f4, z_f, wbd_b):
    b, _, s, _ = u_f4.shape
    n2 = FFT_N2
    n1 = s // n2
    assert s == n1 * n2 and n1 % FFT_K1_STEP == 0, s
    w1, tc, ts, k3, cs = _fft_constants(s)
    a_shape = jax.ShapeDtypeStruct((b, n2, n1, 512), BF16)
    re, mi = pl.pallas_call(
        _fft1_kernel,
        grid=(b, 512 // (FFT_SLABS * LANES)),
        in_specs=[
            pl.BlockSpec((1, FFT_SLABS, s, LANES), lambda i, j: (i, j, 0, 0)),
            pl.BlockSpec((2 * n1, n1), lambda i, j: (0, 0)),
            pl.BlockSpec((n2, n1, LANES), lambda i, j: (0, 0, 0)),
            pl.BlockSpec((n2, n1, LANES), lambda i, j: (0, 0, 0)),
        ],
        out_specs=[
            pl.BlockSpec((1, n2, n1, FFT_SLABS * LANES), lambda i, j: (i, 0, 0, j)),
            pl.BlockSpec((1, n2, n1, FFT_SLABS * LANES), lambda i, j: (i, 0, 0, j)),
        ],
        out_shape=[a_shape, a_shape],
        compiler_params=_cparams(2),
        name="fft1",
    )(u_f4, w1, tc, ts)

    blk = pl.BlockSpec((1, n2, FFT_K1_STEP, 512), lambda i, j: (i, 0, j, 0))
    fixed = lambda i, j: (0, 0)
    scale = 1.0 / math.sqrt(float(s) * F_GROUP)
    y4 = pl.pallas_call(
        functools.partial(_fft3_kernel, scale=scale),
        grid=(b, n1 // FFT_K1_STEP),
        in_specs=[
            blk,
            blk,
            pl.BlockSpec(k3.shape, fixed),
            pl.BlockSpec(cs.shape, fixed),
            pl.BlockSpec((512, 512), fixed),
            blk,
        ],
        out_specs=blk,
        out_shape=jax.ShapeDtypeStruct((b, n2, n1, 512), BF16),
        scratch_shapes=[pltpu.VMEM(cs.shape, BF16)],
        compiler_params=_cparams(2),
        name="fft3",
    )(re, mi, k3, cs, wbd_b, z_f.reshape(b, n2, n1, 512))
    return y4.reshape(b, s, 512)


def _split3_dot(tri, v):
    v1 = v.astype(BF16)
    r1 = v - v1.astype(F32)
    v2 = r1.astype(BF16)
    v3 = (r1 - v2.astype(F32)).astype(BF16)
    d = lambda w: jnp.dot(tri, w, preferred_element_type=F32)
    return d(v1) + d(v2) + d(v3)


N_PANELS = 4
LOG2E = math.log2(math.e)


def _ssd_panels(dt, alog, tri, pan_ref, etot_ref, c, *, forward):
    a = dt * (-jnp.exp(alog))
    acs = _split3_dot(tri, a)
    total = acs[CHUNK - 1 : CHUNK, :] if forward else acs[0:1, :]
    acs2 = acs * LOG2E
    l2dt = jnp.log2(dt)
    pan_ref[c, 0] = acs2
    pan_ref[c, 1] = (acs2 - l2dt).T
    pan_ref[c, 2] = l2dt.T
    pan_ref[c, 3] = (dt * jnp.exp(total - acs)).T
    etot_ref[c] = jnp.broadcast_to(jnp.exp(total), (8, LANES))


def _ssd_chunk(act_ref, r0, pan_ref, etot_ref, ci, state_ref, gb_ref, emit, *, lane0, forward):
    rows = pl.ds(r0, CHUNK)
    e_tot = etot_ref[ci][0:1, :]

    li = lax.broadcasted_iota(jnp.int32, (CHUNK, CHUNK), 0)
    si = lax.broadcasted_iota(jnp.int32, (CHUNK, CHUNK), 1)
    keep = (li >= si) if forward else (li <= si)
    left = lax.broadcasted_iota(jnp.int32, (CHUNK, LANES), 1) < HEAD_DIM
    left_row = lax.broadcasted_iota(jnp.int32, (1, LANES), 1) < HEAD_DIM
    nt_dims = (((1,), (1,)), ((), ()))

    for g in range(N_BC_GROUPS):
        bg = act_ref[0, rows, pl.ds(D_SSM + D_STATE * g, D_STATE)]
        cg = act_ref[0, rows, pl.ds(D_SSM + 512 + D_STATE * g, D_STATE)]
        cb = lax.dot_general(cg, bg, nt_dims, preferred_element_type=F32)
        gb_ref[0] = jnp.where(keep, cb, 0.0).astype(BF16)
        gb_ref[1] = bg.astype(F32).T.astype(BF16)
        y_off = jnp.dot(cg, state_ref[g].astype(BF16), preferred_element_type=F32)
        ys, sts = [], []
        for j in range(HEADS_PER_GROUP // 2):
            pair = (HEADS_PER_GROUP // 2) * g + j
            psl = slice(LANES * j, LANES * (j + 1))
            xp = act_ref[0, rows, pl.ds(LANES * pair, LANES)]
            zero = jnp.zeros_like(xp)
            xbd = jnp.concatenate([jnp.where(left, xp, zero), jnp.where(left, zero, xp)], axis=0)
            gl, wl, es, et = [], [], [], []
            for h in (2 * pair, 2 * pair + 1):
                hl = lane0 + h
                row = lambda p: pan_ref[ci, p, pl.ds(hl, 1), :]
                colb = jnp.broadcast_to(pan_ref[ci, 0][:, hl : hl + 1], (CHUNK, LANES))
                dec = jnp.exp2(jnp.minimum(colb - row(1), row(2)))
                gl.append(dec.astype(BF16) * gb_ref[0])
                wl.append(gb_ref[1] * row(3).astype(BF16))
                es.append(colb)
                et.append(jnp.broadcast_to(e_tot[:, hl : hl + 1], (1, LANES)))
            lhs = jnp.concatenate(
                [jnp.concatenate(gl, axis=1), jnp.concatenate(wl, axis=1)], axis=0
            )
            res = jnp.dot(lhs, xbd, preferred_element_type=F32)
            ys.append(res[:CHUNK] + y_off[:, psl] * jnp.exp2(jnp.where(left, es[0], es[1])))
            sts.append(
                state_ref[g, :, psl] * jnp.where(left_row, et[0], et[1]) + res[CHUNK:]
            )
        state_ref[g] = jnp.concatenate(sts, axis=1)
        emit(g, jnp.concatenate(ys, axis=1))


def _ssd_fwd_kernel(act_ref, dt_ref, alog_ref, tri_ref, yf_ref, state_scr, pan_scr, etot_scr,
                    gb_scr, *, tm):
    @pl.when(pl.program_id(1) == 0)
    def _():
        state_scr[...] = jnp.zeros_like(state_scr)

    for c in range(tm // CHUNK):
        _ssd_panels(dt_ref[0, CHUNK * c : CHUNK * (c + 1), :], alog_ref[...], tri_ref[...],
                    pan_scr, etot_scr, c, forward=True)

    def body(ci, carry):
        r0 = pl.multiple_of(ci * CHUNK, CHUNK)

        def emit(g, y):
            yf_ref[0, pl.ds(r0, CHUNK), pl.ds(GROUP_W * g, GROUP_W)] = y

        _ssd_chunk(act_ref, r0, pan_scr, etot_scr, ci, state_scr, gb_scr, emit, lane0=0,
                   forward=True)
        return carry

    lax.fori_loop(0, tm // CHUNK, body, 0, unroll=SSD_FWD_UNROLL)


@functools.lru_cache(maxsize=None)
def _ssd_constants():
    tri_f = np.tril(np.ones((CHUNK, CHUNK)))
    tri_b = np.triu(np.ones((CHUNK, CHUNK)))
    return jnp.asarray(tri_f, BF16), jnp.asarray(tri_b, BF16)


def _ssd_fwd(act, dt, alog_p, tm):
    b, s, _ = act.shape
    nt = s // tm
    tri_f, _ = _ssd_constants()
    fixed = lambda bi, i: (0, 0)
    tile = lambda w: pl.BlockSpec((1, tm, w), lambda bi, i: (bi, i, 0))
    scratch = [
        pltpu.VMEM((N_BC_GROUPS, D_STATE, GROUP_W), F32),
        pltpu.VMEM((tm // CHUNK, N_PANELS, CHUNK, LANES), F32),
        pltpu.VMEM((tm // CHUNK, 8, LANES), F32),
        pltpu.VMEM((2, CHUNK, CHUNK), BF16),
    ]

    return pl.pallas_call(
        functools.partial(_ssd_fwd_kernel, tm=tm),
        grid=(b, nt),
        in_specs=[
            tile(CONV_DIM),
            tile(LANES),
            pl.BlockSpec((1, LANES), fixed),
            pl.BlockSpec((CHUNK, CHUNK), fixed),
        ],
        out_specs=tile(D_SSM),
        out_shape=jax.ShapeDtypeStruct((b, s, D_SSM), F32),
        scratch_shapes=scratch,
        compiler_params=_cparams(2),
        name="ssd_fwd",
    )(act, dt, alog_p, tri_f)


def _tail_kernel(act_ref, dt_ref, yf_ref, zs_ref, x_ref, yfn_ref, p_ref, alog_ref, tri_ref,
                 dskip_ref, nw_ref, wo_ref, wg_ref, wp_ref, fw_ref, o_ref, state_scr, pan_scr,
                 etot_scr, gb_scr, ys_scr, ysp_scr, *, tm, nt, ntot):
    q = pl.program_id(0)

    @pl.when(q == 0)
    def _():
        ys_scr[...] = jnp.zeros_like(ys_scr)

    ysp_scr[...] = ys_scr[...]

    @pl.when(jnp.minimum(q, ntot - 1) % nt == 0)
    def _():
        state_scr[...] = jnp.zeros_like(state_scr)

    nc = tm // CHUNK
    for c in range(nc):
        _ssd_panels(dt_ref[0, CHUNK * c : CHUNK * (c + 1), :], alog_ref[...], tri_ref[...],
                    pan_scr, etot_scr, c, forward=False)

    for ci in reversed(range(nc)):
        rows = slice(CHUNK * ci, CHUNK * (ci + 1))

        def emit(g, y, rows=rows):
            gsl = slice(GROUP_W * g, GROUP_W * (g + 1))
            xs = act_ref[0, rows, gsl].astype(F32)
            y = y + yf_ref[0, rows, gsl] + dskip_ref[:, gsl] * xs
            y = y * zs_ref[0, rows, gsl].astype(F32)
            ms = jnp.mean(y * y, axis=-1, keepdims=True)
            ys_scr[rows, gsl] = ((y * lax.rsqrt(ms + EPS)) * nw_ref[:, gsl]).astype(BF16)

        _ssd_chunk(act_ref, CHUNK * ci, pan_scr, etot_scr, ci, state_scr, gb_scr, emit,
                   lane0=N_HEADS, forward=False)

    mix = jnp.concatenate([yfn_ref[0], ysp_scr[...]], axis=1)
    h = x_ref[0] + jnp.dot(mix, wo_ref[...], preferred_element_type=F32)
    gate = _sigmoid(jnp.dot(h.astype(BF16), wg_ref[...], preferred_element_type=F32))
    ple = jnp.dot(p_ref[0].astype(BF16), wp_ref[...], preferred_element_type=F32)
    h = h + ple * gate
    ms = jnp.mean(h * h, axis=-1, keepdims=True)
    o_ref[0] = (h * lax.rsqrt(ms + EPS)) * fw_ref[...]


def _tail(act, dt, y_fwd, z_s, x, y_f, p, alog_p, dskip_x, ssd_nw, w_out_b, w_gate_b, w_ple_b,
          final_w, tm):
    b, s, _ = act.shape
    nt = s // tm
    ntot = b * nt
    _, tri_b = _ssd_constants()

    def scan_tile(q):
        qs = jnp.minimum(q, ntot - 1)
        return qs // nt, nt - 1 - qs % nt, 0

    def epi_tile(q):
        return scan_tile(jnp.maximum(q - 1, 0))

    stile = lambda w: pl.BlockSpec((1, tm, w), scan_tile)
    etile = lambda w: pl.BlockSpec((1, tm, w), epi_tile)
    fixed = lambda shape: pl.BlockSpec(shape, lambda q: (0, 0))
    return pl.pallas_call(
        functools.partial(_tail_kernel, tm=tm, nt=nt, ntot=ntot),
        grid=(ntot + 1,),
        in_specs=[
            stile(CONV_DIM), stile(LANES), stile(D_SSM), stile(D_SSM),
            etile(D_MODEL), etile(512), etile(D_PLE),
            fixed((1, LANES)), fixed((CHUNK, CHUNK)), fixed((1, D_SSM)), fixed((1, D_SSM)),
            fixed((2048, D_MODEL)), fixed((D_MODEL, D_MODEL)), fixed((D_PLE, D_MODEL)),
            fixed((1, D_MODEL)),
        ],
        out_specs=etile(D_MODEL),
        out_shape=jax.ShapeDtypeStruct((b, s, D_MODEL), F32),
        scratch_shapes=[
            pltpu.VMEM((N_BC_GROUPS, D_STATE, GROUP_W), F32),
            pltpu.VMEM((tm // CHUNK, N_PANELS, CHUNK, LANES), F32),
            pltpu.VMEM((tm // CHUNK, 8, LANES), F32),
            pltpu.VMEM((2, CHUNK, CHUNK), BF16),
            pltpu.VMEM((tm, D_SSM), BF16),
            pltpu.VMEM((tm, D_SSM), BF16),
        ],
        compiler_params=_cparams(1),
        name="tail",
    )(act, dt, y_fwd, z_s, x, y_f, p, alog_p, tri_b, dskip_x, ssd_nw, w_out_b, w_gate_b,
      w_ple_b, final_w)


def _trunk(x, p, norm_w, w_in, w_fmix, conv_w, conv_b, a_log_f, a_log_b, dt_bias_f,
           dt_bias_b, d_skip, ssd_norm_w, w_out, w_ple_in, w_ple_gate, final_norm_w):
    b, s, _ = x.shape
    t = b * s
    tm = 512
    assert s % tm == 0, s

    pad = jnp.zeros((LANES - 2 * N_HEADS,), F32)
    bias_p = jnp.concatenate([dt_bias_f[0], dt_bias_b[0], pad])[None, :]
    alog_p = jnp.concatenate([a_log_f[0], a_log_b[0], pad])[None, :]
    dskip_x = jnp.repeat(d_skip[0], HEAD_DIM)[None, :]

    w_in_b = jnp.pad(w_in[0].astype(BF16), ((0, 0), (0, D_IN_PAD - D_IN_PROJ)))
    z_f, u_f, z_s, act, dt = _in_proj(
        x, norm_w[0][None, :], w_in_b, conv_w[0], conv_b[0][None, :], bias_p, 2 * tm
    )

    wbd = jax.scipy.linalg.block_diag(*[w_fmix[0, g] for g in range(N_FGROUPS)]).astype(BF16)
    y_f = _fnet(u_f, z_f, wbd)

    y_fwd = _ssd_fwd(act, dt, alog_p, 4 * tm)
    return _tail(
        act, dt, y_fwd, z_s, x, y_f, p[0], alog_p, dskip_x, ssd_norm_w[0][None, :],
        w_out[0].astype(BF16), w_ple_gate[0].astype(BF16), w_ple_in[0].astype(BF16),
        final_norm_w[None, :], tm,
    )


def kernel(x_prompt, x_sample, p_prompt, p_sample, norm_w, w_in, w_fmix, conv_w, conv_b, a_log_f, a_log_b, dt_bias_f, dt_bias_b, d_skip, ssd_norm_w, w_out, w_ple_in, w_ple_gate, final_norm_w):
    weights = (norm_w, w_in, w_fmix, conv_w, conv_b, a_log_f, a_log_b, dt_bias_f, dt_bias_b,
               d_skip, ssd_norm_w, w_out, w_ple_in, w_ple_gate, final_norm_w)
    return (_trunk(x_prompt, p_prompt, *weights), _trunk(x_sample, p_sample, *weights))
```

```python
import functools
import math

import jax
import jax.numpy as jnp
import numpy as np
from jax import lax
from jax.experimental import pallas as pl
from jax.experimental.pallas import tpu as pltpu

F32 = jnp.float32
BF16 = jnp.bfloat16

D_MODEL = 1024
D_FOURIER = 512
N_FGROUPS = 8
F_GROUP = 64
D_SSM = 1536
HEAD_DIM = 64
N_HEADS = 24
N_BC_GROUPS = 4
HEADS_PER_GROUP = 6
GROUP_W = HEADS_PER_GROUP * HEAD_DIM
D_STATE = 128
D_CONV = 5
CONV_DIM = 2560
CHUNK = 128
D_PLE = 256
D_IN_PROJ = 5168
D_IN_PAD = 5248
DT_COL = 5120
EPS = 1e-6

LANES = 128
HALO = 8
SSD_FWD_UNROLL = 8
FFT_N2 = 16
FFT_K1_TILE = 16
FFT_K1_STEP = 64
FFT_SLABS = 2
VMEM_LIMIT = 56 * 1024 * 1024


def _cparams(n_axes):
    return pltpu.CompilerParams(
        dimension_semantics=("arbitrary",) * n_axes, vmem_limit_bytes=VMEM_LIMIT
    )


def _sigmoid(v):
    return 0.5 + 0.5 * jnp.tanh(0.5 * v)


def _silu(v):
    hv = 0.5 * v
    return hv + hv * jnp.tanh(hv)


def _bf16_bits_high(v):
    return lax.bitcast_convert_type(v.astype(BF16).astype(F32), jnp.uint32)


def _in_proj_kernel(xp_ref, x_ref, xn_ref, nw_ref, w_ref, cw_ref, cb_ref, dtb_ref,
                    zf_ref, uf_ref, zs_ref, act_ref, dt_ref, xe_scr, *, tm):
    i = pl.program_id(1)
    nt = pl.num_programs(1)

    def norm(x):
        ms = jnp.mean(x * x, axis=-1, keepdims=True)
        return (x * lax.rsqrt(ms + EPS)) * nw_ref[...]

    u32 = norm(x_ref[0])
    up = jnp.where(i == 0, 0.0, norm(xp_ref[0]))
    un = jnp.where(i == nt - 1, 0.0, norm(xn_ref[0]))
    u = u32.astype(BF16)
    u_ext = jnp.concatenate([up, u32, un], axis=0).astype(BF16)

    def seg(lhs, c0, width):
        return jnp.dot(lhs, w_ref[:, c0 : c0 + width], preferred_element_type=F32)

    mid = D_CONV // 2

    def conv_block(j):
        xe = seg(u_ext, 2560 + 512 * j, 512)
        for c in range(512 // LANES):
            xe_scr[c] = xe[:, LANES * c : LANES * (c + 1)]
        for c in range(512 // LANES):
            cols = slice(512 * j + LANES * c, 512 * j + LANES * (c + 1))
            taps = [
                xe_scr[c, pl.ds(HALO - mid + i, tm // 2, stride=2), :] for i in range(D_CONV + 1)
            ]
            halves = []
            for par in range(2):
                acc = cb_ref[:, cols]
                for k in range(D_CONV):
                    acc = acc + taps[k + par] * cw_ref[k : k + 1, cols]
                halves.append(_bf16_bits_high(_silu(acc)))
            words = (halves[0] >> 16) | halves[1]
            act_ref[0, :, cols] = pltpu.bitcast(words, BF16)

    def z_block(j):
        v = seg(u, 512 * j, 512)
        if j == 0:
            zf_ref[0] = v.astype(BF16)
        elif j == 1:
            for c in range(512 // LANES):
                uf_ref[0, c] = v[:, LANES * c : LANES * (c + 1)]
        else:
            zs_ref[0, :, 512 * (j - 2) : 512 * (j - 1)] = _silu(v).astype(BF16)

    for j in range(CONV_DIM // 512):
        conv_block(j)
        z_block(j)
    dt_pre = seg(u, DT_COL, LANES) + dtb_ref[...]
    dt_ref[0] = jnp.maximum(dt_pre, 0.0) + jnp.log1p(jnp.exp(-jnp.abs(dt_pre)))


def _in_proj(x, norm_w, w_in_b, conv_w, conv_b, dt_bias_p, tm):
    b, s, _ = x.shape
    tile = lambda w: pl.BlockSpec((1, tm, w), lambda bi, i: (bi, i, 0))
    fixed = lambda bi, i: (0, 0)
    outs = [(512, BF16), None, (D_SSM, BF16), (CONV_DIM, BF16), (LANES, F32)]
    out_specs = [tile(o[0]) if o else None for o in outs]
    out_shape = [jax.ShapeDtypeStruct((b, s, o[0]), o[1]) if o else None for o in outs]
    out_specs[1] = pl.BlockSpec((1, 512 // LANES, tm, LANES), lambda bi, i: (bi, 0, i, 0))
    out_shape[1] = jax.ShapeDtypeStruct((b, 512 // LANES, s, LANES), F32)
    return pl.pallas_call(
        functools.partial(_in_proj_kernel, tm=tm),
        grid=(b, s // tm),
        in_specs=[
            pl.BlockSpec(
                (1, HALO, D_MODEL),
                lambda bi, i: (bi, jnp.maximum(i * (tm // HALO) - 1, 0), 0),
            ),
            tile(D_MODEL),
            pl.BlockSpec(
                (1, HALO, D_MODEL),
                lambda bi, i: (bi, jnp.minimum((i + 1) * (tm // HALO), s // HALO - 1), 0),
            ),
            pl.BlockSpec((1, D_MODEL), fixed),
            pl.BlockSpec((D_MODEL, D_IN_PAD), fixed),
            pl.BlockSpec((D_CONV, CONV_DIM), fixed),
            pl.BlockSpec((1, CONV_DIM), fixed),
            pl.BlockSpec((1, LANES), fixed),
        ],
        out_specs=out_specs,
        out_shape=out_shape,
        scratch_shapes=[pltpu.VMEM((512 // LANES, tm + 2 * HALO, LANES), F32)],
        compiler_params=_cparams(2),
        name="in_proj",
    )(x, x, x, norm_w, w_in_b, conv_w, conv_b, dt_bias_p)


def _fft1_kernel(u_ref, w1_ref, tc_ref, ts_ref, re_ref, mi_ref):
    n1 = re_ref.shape[2]
    w1 = w1_ref[...]
    for s2 in range(FFT_N2):
        x = jnp.concatenate(
            [u_ref[0, c, pl.ds(s2, n1, stride=FFT_N2), :] for c in range(FFT_SLABS)], axis=1
        )
        pq = jnp.dot(w1, x.astype(BF16), preferred_element_type=F32)
        p = pq[:n1]
        qm = pq[n1:]
        c = jnp.concatenate([tc_ref[s2]] * FFT_SLABS, axis=1)
        s = jnp.concatenate([ts_ref[s2]] * FFT_SLABS, axis=1)
        re_ref[0, s2] = (p * c - qm * s).astype(BF16)
        mi_ref[0, s2] = (p * s + qm * c).astype(BF16)


def _fft3_kernel(re_ref, mi_ref, k3_ref, cs_ref, wbd_ref, zf_ref, y_ref, mix_scr, *, scale):
    @pl.when((pl.program_id(0) == 0) & (pl.program_id(1) == 0))
    def _():
        mix = jnp.dot(cs_ref[...], wbd_ref[...], preferred_element_type=F32) * scale
        mix_scr[...] = mix.astype(BF16)

    kt = FFT_K1_TILE
    rows = FFT_N2 * kt
    zs = []
    for t in range(FFT_K1_STEP // kt):
        sl = slice(kt * t, kt * (t + 1))
        rhs = jnp.concatenate(
            [re_ref[0, :, sl, :].reshape(rows, 512), mi_ref[0, :, sl, :].reshape(rows, 512)],
            axis=0,
        )
        z = jnp.dot(k3_ref[...], rhs, preferred_element_type=F32)
        zs.append(jnp.concatenate([z[:rows], z[rows:]], axis=1).astype(BF16))
    y = jnp.dot(jnp.concatenate(zs, axis=0), mix_scr[...], preferred_element_type=F32)
    for t in range(FFT_K1_STEP // kt):
        sl = slice(kt * t, kt * (t + 1))
        zf = zf_ref[0, :, sl, :].reshape(rows, 512).astype(F32)
        yt = y[rows * t : rows * (t + 1)] * _silu(zf)
        y_ref[0, :, sl, :] = yt.astype(BF16).reshape(FFT_N2, kt, 512)


def _dft_cos_sin(n):
    j = np.arange(n, dtype=np.int64)
    ang = 2.0 * np.pi * ((j[:, None] * j[None, :]) % n).astype(np.float64) / n
    return np.cos(ang), np.sin(ang)


@functools.lru_cache(maxsize=None)
def _fft_constants(s):
    n2 = FFT_N2
    n1 = s // n2
    c1, s1 = _dft_cos_sin(n1)
    w1 = np.concatenate([c1, s1], axis=0)
    k1 = np.arange(n1, dtype=np.int64)
    s2 = np.arange(n2, dtype=np.int64)
    ang = 2.0 * np.pi * ((s2[:, None] * k1[None, :]) % s).astype(np.float64) / s
    tc = np.repeat(np.cos(ang)[:, :, None], LANES, axis=2)
    ts = np.repeat(np.sin(ang)[:, :, None], LANES, axis=2)
    c2, s2m = _dft_cos_sin(n2)
    eye = np.eye(FFT_K1_TILE)
    kc = np.kron(c2, eye)
    ks = np.kron(s2m, eye)
    k3 = np.block([[kc, -ks], [ks, kc]])
    cc, sc = _dft_cos_sin(F_GROUP)
    eye_g = np.eye(N_FGROUPS)
    cs = np.concatenate([np.kron(eye_g, cc), -np.kron(eye_g, sc)], axis=0)
    return (
        jnp.asarray(w1, BF16),
        jnp.asarray(tc, F32),
        jnp.asarray(ts, F32),
        jnp.asarray(k3, BF16),
        jnp.asarray(cs, BF16),
    )


def _fnet(u_f4, z_f, wbd_b):
    b, _, s, _ = u_f4.shape
    n2 = FFT_N2
    n1 = s // n2
    assert s == n1 * n2 and n1 % FFT_K1_STEP == 0, s
    w1, tc, ts, k3, cs = _fft_constants(s)
    a_shape = jax.ShapeDtypeStruct((b, n2, n1, 512), BF16)
    re, mi = pl.pallas_call(
        _fft1_kernel,
        grid=(b, 512 // (FFT_SLABS * LANES)),
        in_specs=[
            pl.BlockSpec((1, FFT_SLABS, s, LANES), lambda i, j: (i, j, 0, 0)),
            pl.BlockSpec((2 * n1, n1), lambda i, j: (0, 0)),
            pl.BlockSpec((n2, n1, LANES), lambda i, j: (0, 0, 0)),
            pl.BlockSpec((n2, n1, LANES), lambda i, j: (0, 0, 0)),
        ],
        out_specs=[
            pl.BlockSpec((1, n2, n1, FFT_SLABS * LANES), lambda i, j: (i, 0, 0, j)),
            pl.BlockSpec((1, n2, n1, FFT_SLABS * LANES), lambda i, j: (i, 0, 0, j)),
        ],
        out_shape=[a_shape, a_shape],
        compiler_params=_cparams(2),
        name="fft1",
    )(u_f4, w1, tc, ts)

    blk = pl.BlockSpec((1, n2, FFT_K1_STEP, 512), lambda i, j: (i, 0, j, 0))
    fixed = lambda i, j: (0, 0)
    scale = 1.0 / math.sqrt(float(s) * F_GROUP)
    y4 = pl.pallas_call(
        functools.partial(_fft3_kernel, scale=scale),
        grid=(b, n1 // FFT_K1_STEP),
        in_specs=[
            blk,
            blk,
            pl.BlockSpec(k3.shape, fixed),
            pl.BlockSpec(cs.shape, fixed),
            pl.BlockSpec((512, 512), fixed),
            blk,
        ],
        out_specs=blk,
        out_shape=jax.ShapeDtypeStruct((b, n2, n1, 512), BF16),
        scratch_shapes=[pltpu.VMEM(cs.shape, BF16)],
        compiler_params=_cparams(2),
        name="fft3",
    )(re, mi, k3, cs, wbd_b, z_f.reshape(b, n2, n1, 512))
    return y4.reshape(b, s, 512)


def _split3_dot(tri, v):
    v1 = v.astype(BF16)
    r1 = v - v1.astype(F32)
    v2 = r1.astype(BF16)
    v3 = (r1 - v2.astype(F32)).astype(BF16)
    d = lambda w: jnp.dot(tri, w, preferred_element_type=F32)
    return d(v1) + d(v2) + d(v3)


N_PANELS = 4
LOG2E = math.log2(math.e)


def _ssd_panels(dt, alog, tri, pan_ref, etot_ref, c, *, forward):
    a = dt * (-jnp.exp(alog))
    acs = _split3_dot(tri, a)
    total = acs[CHUNK - 1 : CHUNK, :] if forward else acs[0:1, :]
    acs2 = acs * LOG2E
    l2dt = jnp.log2(dt)
    pan_ref[c, 0] = acs2
    pan_ref[c, 1] = (acs2 - l2dt).T
    pan_ref[c, 2] = l2dt.T
    pan_ref[c, 3] = (dt * jnp.exp(total - acs)).T
    etot_ref[c] = jnp.broadcast_to(jnp.exp(total), (8, LANES))


def _ssd_chunk(act_ref, r0, pan_ref, etot_ref, ci, state_ref, gb_ref, emit, *, lane0, forward):
    rows = pl.ds(r0, CHUNK)
    e_tot = etot_ref[ci][0:1, :]

    li = lax.broadcasted_iota(jnp.int32, (CHUNK, CHUNK), 0)
    si = lax.broadcasted_iota(jnp.int32, (CHUNK, CHUNK), 1)
    keep = (li >= si) if forward else (li <= si)
    left = lax.broadcasted_iota(jnp.int32, (CHUNK, LANES), 1) < HEAD_DIM
    left_row = lax.broadcasted_iota(jnp.int32, (1, LANES), 1) < HEAD_DIM
    nt_dims = (((1,), (1,)), ((), ()))

    for g in range(N_BC_GROUPS):
        bg = act_ref[0, rows, pl.ds(D_SSM + D_STATE * g, D_STATE)]
        cg = act_ref[0, rows, pl.ds(D_SSM + 512 + D_STATE * g, D_STATE)]
        cb = lax.dot_general(cg, bg, nt_dims, preferred_element_type=F32)
        gb_ref[0] = jnp.where(keep, cb, 0.0).astype(BF16)
        gb_ref[1] = bg.astype(F32).T.astype(BF16)
        y_off = jnp.dot(cg, state_ref[g].astype(BF16), preferred_element_type=F32)
        ys, sts = [], []
        for j in range(HEADS_PER_GROUP // 2):
            pair = (HEADS_PER_GROUP // 2) * g + j
            psl = slice(LANES * j, LANES * (j + 1))
            xp = act_ref[0, rows, pl.ds(LANES * pair, LANES)]
            zero = jnp.zeros_like(xp)
            xbd = jnp.concatenate([jnp.where(left, xp, zero), jnp.where(left, zero, xp)], axis=0)
            gl, wl, es, et = [], [], [], []
            for h in (2 * pair, 2 * pair + 1):
                hl = lane0 + h
                row = lambda p: pan_ref[ci, p, pl.ds(hl, 1), :]
                colb = jnp.broadcast_to(pan_ref[ci, 0][:, hl : hl + 1], (CHUNK, LANES))
                dec = jnp.exp2(jnp.minimum(colb - row(1), row(2)))
                gl.append(dec.astype(BF16) * gb_ref[0])
                wl.append(gb_ref[1] * row(3).astype(BF16))
                es.append(colb)
                et.append(jnp.broadcast_to(e_tot[:, hl : hl + 1], (1, LANES)))
            lhs = jnp.concatenate(
                [jnp.concatenate(gl, axis=1), jnp.concatenate(wl, axis=1)], axis=0
            )
            res = jnp.dot(lhs, xbd, preferred_element_type=F32)
            ys.append(res[:CHUNK] + y_off[:, psl] * jnp.exp2(jnp.where(left, es[0], es[1])))
            sts.append(
                state_ref[g, :, psl] * jnp.where(left_row, et[0], et[1]) + res[CHUNK:]
            )
        state_ref[g] = jnp.concatenate(sts, axis=1)
        emit(g, jnp.concatenate(ys, axis=1))


def _ssd_fwd_kernel(act_ref, dt_ref, alog_ref, tri_ref, yf_ref, state_scr, pan_scr, etot_scr,
                    gb_scr, *, tm):
    @pl.when(pl.program_id(1) == 0)
    def _():
        state_scr[...] = jnp.zeros_like(state_scr)

    for c in range(tm // CHUNK):
        _ssd_panels(dt_ref[0, CHUNK * c : CHUNK * (c + 1), :], alog_ref[...], tri_ref[...],
                    pan_scr, etot_scr, c, forward=True)

    def body(ci, carry):
        r0 = pl.multiple_of(ci * CHUNK, CHUNK)

        def emit(g, y):
            yf_ref[0, pl.ds(r0, CHUNK), pl.ds(GROUP_W * g, GROUP_W)] = y

        _ssd_chunk(act_ref, r0, pan_scr, etot_scr, ci, state_scr, gb_scr, emit, lane0=0,
                   forward=True)
        return carry

    lax.fori_loop(0, tm // CHUNK, body, 0, unroll=SSD_FWD_UNROLL)


@functools.lru_cache(maxsize=None)
def _ssd_constants():
    tri_f = np.tril(np.ones((CHUNK, CHUNK)))
    tri_b = np.triu(np.ones((CHUNK, CHUNK)))
    return jnp.asarray(tri_f, BF16), jnp.asarray(tri_b, BF16)


def _ssd_fwd(act, dt, alog_p, tm):
    b, s, _ = act.shape
    nt = s // tm
    tri_f, _ = _ssd_constants()
    fixed = lambda bi, i: (0, 0)
    tile = lambda w: pl.BlockSpec((1, tm, w), lambda bi, i: (bi, i, 0))
    scratch = [
        pltpu.VMEM((N_BC_GROUPS, D_STATE, GROUP_W), F32),
        pltpu.VMEM((tm // CHUNK, N_PANELS, CHUNK, LANES), F32),
        pltpu.VMEM((tm // CHUNK, 8, LANES), F32),
        pltpu.VMEM((2, CHUNK, CHUNK), BF16),
    ]

    return pl.pallas_call(
        functools.partial(_ssd_fwd_kernel, tm=tm),
        grid=(b, nt),
        in_specs=[
            tile(CONV_DIM),
            tile(LANES),
            pl.BlockSpec((1, LANES), fixed),
            pl.BlockSpec((CHUNK, CHUNK), fixed),
        ],
        out_specs=tile(D_SSM),
        out_shape=jax.ShapeDtypeStruct((b, s, D_SSM), F32),
        scratch_shapes=scratch,
        compiler_params=_cparams(2),
        name="ssd_fwd",
    )(act, dt, alog_p, tri_f)


def _tail_kernel(act_ref, dt_ref, yf_ref, zs_ref, x_ref, yfn_ref, p_ref, alog_ref, tri_ref,
                 dskip_ref, nw_ref, wo_ref, wg_ref, wp_ref, fw_ref, o_ref, state_scr, pan_scr,
                 etot_scr, gb_scr, ys_scr, ysp_scr, *, tm, nt, ntot):
    q = pl.program_id(0)

    @pl.when(q == 0)
    def _():
        ys_scr[...] = jnp.zeros_like(ys_scr)

    ysp_scr[...] = ys_scr[...]

    @pl.when(jnp.minimum(q, ntot - 1) % nt == 0)
    def _():
        state_scr[...] = jnp.zeros_like(state_scr)

    nc = tm // CHUNK
    for c in range(nc):
        _ssd_panels(dt_ref[0, CHUNK * c : CHUNK * (c + 1), :], alog_ref[...], tri_ref[...],
                    pan_scr, etot_scr, c, forward=False)

    for ci in reversed(range(nc)):
        rows = slice(CHUNK * ci, CHUNK * (ci + 1))

        def emit(g, y, rows=rows):
            gsl = slice(GROUP_W * g, GROUP_W * (g + 1))
            xs = act_ref[0, rows, gsl].astype(F32)
            y = y + yf_ref[0, rows, gsl] + dskip_ref[:, gsl] * xs
            y = y * zs_ref[0, rows, gsl].astype(F32)
            ms = jnp.mean(y * y, axis=-1, keepdims=True)
            ys_scr[rows, gsl] = ((y * lax.rsqrt(ms + EPS)) * nw_ref[:, gsl]).astype(BF16)

        _ssd_chunk(act_ref, CHUNK * ci, pan_scr, etot_scr, ci, state_scr, gb_scr, emit,
                   lane0=N_HEADS, forward=False)

    mix = jnp.concatenate([yfn_ref[0], ysp_scr[...]], axis=1)
    h = x_ref[0] + jnp.dot(mix, wo_ref[...], preferred_element_type=F32)
    gate = _sigmoid(jnp.dot(h.astype(BF16), wg_ref[...], preferred_element_type=F32))
    ple = jnp.dot(p_ref[0].astype(BF16), wp_ref[...], preferred_element_type=F32)
    h = h + ple * gate
    ms = jnp.mean(h * h, axis=-1, keepdims=True)
    o_ref[0] = (h * lax.rsqrt(ms + EPS)) * fw_ref[...]


def _tail(act, dt, y_fwd, z_s, x, y_f, p, alog_p, dskip_x, ssd_nw, w_out_b, w_gate_b, w_ple_b,
          final_w, tm):
    b, s, _ = act.shape
    nt = s // tm
    ntot = b * nt
    _, tri_b = _ssd_constants()

    def scan_tile(q):
        qs = jnp.minimum(q, ntot - 1)
        return qs // nt, nt - 1 - qs % nt, 0

    def epi_tile(q):
        return scan_tile(jnp.maximum(q - 1, 0))

    stile = lambda w: pl.BlockSpec((1, tm, w), scan_tile)
    etile = lambda w: pl.BlockSpec((1, tm, w), epi_tile)
    fixed = lambda shape: pl.BlockSpec(shape, lambda q: (0, 0))
    return pl.pallas_call(
        functools.partial(_tail_kernel, tm=tm, nt=nt, ntot=ntot),
        grid=(ntot + 1,),
        in_specs=[
            stile(CONV_DIM), stile(LANES), stile(D_SSM), stile(D_SSM),
            etile(D_MODEL), etile(512), etile(D_PLE),
            fixed((1, LANES)), fixed((CHUNK, CHUNK)), fixed((1, D_SSM)), fixed((1, D_SSM)),
            fixed((2048, D_MODEL)), fixed((D_MODEL, D_MODEL)), fixed((D_PLE, D_MODEL)),
            fixed((1, D_MODEL)),
        ],
        out_specs=etile(D_MODEL),
        out_shape=jax.ShapeDtypeStruct((b, s, D_MODEL), F32),
        scratch_shapes=[
            pltpu.VMEM((N_BC_GROUPS, D_STATE, GROUP_W), F32),
            pltpu.VMEM((tm // CHUNK, N_PANELS, CHUNK, LANES), F32),
            pltpu.VMEM((tm // CHUNK, 8, LANES), F32),
            pltpu.VMEM((2, CHUNK, CHUNK), BF16),
            pltpu.VMEM((tm, D_SSM), BF16),
            pltpu.VMEM((tm, D_SSM), BF16),
        ],
        compiler_params=_cparams(1),
        name="tail",
    )(act, dt, y_fwd, z_s, x, y_f, p, alog_p, tri_b, dskip_x, ssd_nw, w_out_b, w_gate_b,
      w_ple_b, final_w)


def _trunk(x, p, norm_w, w_in, w_fmix, conv_w, conv_b, a_log_f, a_log_b, dt_bias_f,
           dt_bias_b, d_skip, ssd_norm_w, w_out, w_ple_in, w_ple_gate, final_norm_w):
    b, s, _ = x.shape
    t = b * s
    tm = 512
    assert s % tm == 0, s

    pad = jnp.zeros((LANES - 2 * N_HEADS,), F32)
    bias_p = jnp.concatenate([dt_bias_f[0], dt_bias_b[0], pad])[None, :]
    alog_p = jnp.concatenate([a_log_f[0], a_log_b[0], pad])[None, :]
    dskip_x = jnp.repeat(d_skip[0], HEAD_DIM)[None, :]

    w_in_b = jnp.pad(w_in[0].astype(BF16), ((0, 0), (0, D_IN_PAD - D_IN_PROJ)))
    z_f, u_f, z_s, act, dt = _in_proj(
        x, norm_w[0][None, :], w_in_b, conv_w[0], conv_b[0][None, :], bias_p, tm
    )

    wbd = jax.scipy.linalg.block_diag(*[w_fmix[0, g] for g in range(N_FGROUPS)]).astype(BF16)
    y_f = _fnet(u_f, z_f, wbd)

    y_fwd = _ssd_fwd(act, dt, alog_p, 2 * tm)
    return _tail(
        act, dt, y_fwd, z_s, x, y_f, p[0], alog_p, dskip_x, ssd_norm_w[0][None, :],
        w_out[0].astype(BF16), w_ple_gate[0].astype(BF16), w_ple_in[0].astype(BF16),
        final_norm_w[None, :], tm,
    )


def kernel(x_prompt, x_sample, p_prompt, p_sample, norm_w, w_in, w_fmix, conv_w, conv_b, a_log_f, a_log_b, dt_bias_f, dt_bias_b, d_skip, ssd_norm_w, w_out, w_ple_in, w_ple_gate, final_norm_w):
    weights = (norm_w, w_in, w_fmix, conv_w, conv_b, a_log_f, a_log_b, dt_bias_f, dt_bias_b,
               d_skip, ssd_norm_w, w_out, w_ple_in, w_ple_gate, final_norm_w)
    return (_trunk(x_prompt, p_prompt, *weights), _trunk(x_sample, p_sample, *weights))
```

```python
import functools
import math

import jax
import jax.numpy as jnp
import numpy as np
from jax import lax
from jax.experimental import pallas as pl
from jax.experimental.pallas import tpu as pltpu

F32 = jnp.float32
BF16 = jnp.bfloat16

D_MODEL = 1024
D_FOURIER = 512
N_FGROUPS = 8
F_GROUP = 64
D_SSM = 1536
HEAD_DIM = 64
N_HEADS = 24
N_BC_GROUPS = 4
HEADS_PER_GROUP = 6
GROUP_W = HEADS_PER_GROUP * HEAD_DIM
D_STATE = 128
D_CONV = 5
CONV_DIM = 2560
CHUNK = 128
D_PLE = 256
D_IN_PROJ = 5168
D_IN_PAD = 5248
DT_COL = 5120
EPS = 1e-6

LANES = 128
HALO = 8
SSD_FWD_UNROLL = 8
FFT_N2 = 16
FFT_K1_TILE = 16
FFT_K1_STEP = 64
FFT_SLABS = 2
VMEM_LIMIT = 56 * 1024 * 1024


def _cparams(n_axes):
    return pltpu.CompilerParams(
        dimension_semantics=("arbitrary",) * n_axes, vmem_limit_bytes=VMEM_LIMIT
    )


def _sigmoid(v):
    return 0.5 + 0.5 * jnp.tanh(0.5 * v)


def _silu(v):
    hv = 0.5 * v
    return hv + hv * jnp.tanh(hv)


def _bf16_bits_high(v):
    return lax.bitcast_convert_type(v.astype(BF16).astype(F32), jnp.uint32)


def _in_proj_kernel(xp_ref, x_ref, xn_ref, nw_ref, w_ref, cw_ref, cb_ref, dtb_ref,
                    zf_ref, uf_ref, zs_ref, act_ref, dt_ref, xe_scr, *, tm):
    i = pl.program_id(1)
    nt = pl.num_programs(1)

    def norm(x):
        ms = jnp.mean(x * x, axis=-1, keepdims=True)
        return (x * lax.rsqrt(ms + EPS)) * nw_ref[...]

    u32 = norm(x_ref[0])
    up = jnp.where(i == 0, 0.0, norm(xp_ref[0]))
    un = jnp.where(i == nt - 1, 0.0, norm(xn_ref[0]))
    u = u32.astype(BF16)
    u_ext = jnp.concatenate([up, u32, un], axis=0).astype(BF16)

    def seg(lhs, c0, width):
        return jnp.dot(lhs, w_ref[:, c0 : c0 + width], preferred_element_type=F32)

    mid = D_CONV // 2

    def conv_block(j):
        xe = seg(u_ext, 2560 + 512 * j, 512)
        for c in range(512 // LANES):
            xe_scr[c] = xe[:, LANES * c : LANES * (c + 1)]
        for c in range(512 // LANES):
            cols = slice(512 * j + LANES * c, 512 * j + LANES * (c + 1))
            taps = [
                xe_scr[c, pl.ds(HALO - mid + i, tm // 2, stride=2), :] for i in range(D_CONV + 1)
            ]
            halves = []
            for par in range(2):
                acc = cb_ref[:, cols]
                for k in range(D_CONV):
                    acc = acc + taps[k + par] * cw_ref[k : k + 1, cols]
                halves.append(_bf16_bits_high(_silu(acc)))
            words = (halves[0] >> 16) | halves[1]
            act_ref[0, :, cols] = pltpu.bitcast(words, BF16)

    def z_block(j):
        v = seg(u, 512 * j, 512)
        if j == 0:
            zf_ref[0] = v.astype(BF16)
        elif j == 1:
            for c in range(512 // LANES):
                uf_ref[0, c] = v[:, LANES * c : LANES * (c + 1)]
        else:
            zs_ref[0, :, 512 * (j - 2) : 512 * (j - 1)] = _silu(v).astype(BF16)

    for j in range(CONV_DIM // 512):
        conv_block(j)
        z_block(j)
    dt_pre = seg(u, DT_COL, LANES) + dtb_ref[...]
    dt_ref[0] = jnp.maximum(dt_pre, 0.0) + jnp.log1p(jnp.exp(-jnp.abs(dt_pre)))


def _in_proj(x, norm_w, w_in_b, conv_w, conv_b, dt_bias_p, tm):
    b, s, _ = x.shape
    tile = lambda w: pl.BlockSpec((1, tm, w), lambda bi, i: (bi, i, 0))
    fixed = lambda bi, i: (0, 0)
    outs = [(512, BF16), None, (D_SSM, BF16), (CONV_DIM, BF16), (LANES, F32)]
    out_specs = [tile(o[0]) if o else None for o in outs]
    out_shape = [jax.ShapeDtypeStruct((b, s, o[0]), o[1]) if o else None for o in outs]
    out_specs[1] = pl.BlockSpec((1, 512 // LANES, tm, LANES), lambda bi, i: (bi, 0, i, 0))
    out_shape[1] = jax.ShapeDtypeStruct((b, 512 // LANES, s, LANES), F32)
    return pl.pallas_call(
        functools.partial(_in_proj_kernel, tm=tm),
        grid=(b, s // tm),
        in_specs=[
            pl.BlockSpec(
                (1, HALO, D_MODEL),
                lambda bi, i: (bi, jnp.maximum(i * (tm // HALO) - 1, 0), 0),
            ),
            tile(D_MODEL),
            pl.BlockSpec(
                (1, HALO, D_MODEL),
                lambda bi, i: (bi, jnp.minimum((i + 1) * (tm // HALO), s // HALO - 1), 0),
            ),
            pl.BlockSpec((1, D_MODEL), fixed),
            pl.BlockSpec((D_MODEL, D_IN_PAD), fixed),
            pl.BlockSpec((D_CONV, CONV_DIM), fixed),
            pl.BlockSpec((1, CONV_DIM), fixed),
            pl.BlockSpec((1, LANES), fixed),
        ],
        out_specs=out_specs,
        out_shape=out_shape,
        scratch_shapes=[pltpu.VMEM((512 // LANES, tm + 2 * HALO, LANES), F32)],
        compiler_params=_cparams(2),
        name="in_proj",
    )(x, x, x, norm_w, w_in_b, conv_w, conv_b, dt_bias_p)


def _fft1_kernel(u_ref, w1_ref, tc_ref, ts_ref, re_ref, mi_ref):
    n1 = re_ref.shape[2]
    w1 = w1_ref[...]
    for s2 in range(FFT_N2):
        x = jnp.concatenate(
            [u_ref[0, c, pl.ds(s2, n1, stride=FFT_N2), :] for c in range(FFT_SLABS)], axis=1
        )
        pq = jnp.dot(w1, x.astype(BF16), preferred_element_type=F32)
        p = pq[:n1]
        qm = pq[n1:]
        c = jnp.concatenate([tc_ref[s2]] * FFT_SLABS, axis=1)
        s = jnp.concatenate([ts_ref[s2]] * FFT_SLABS, axis=1)
        re_ref[0, s2] = (p * c - qm * s).astype(BF16)
        mi_ref[0, s2] = (p * s + qm * c).astype(BF16)


def _fft3_kernel(re_ref, mi_ref, k3_ref, cs_ref, wbd_ref, zf_ref, y_ref, mix_scr, *, scale):
    @pl.when((pl.program_id(0) == 0) & (pl.program_id(1) == 0))
    def _():
        mix = jnp.dot(cs_ref[...], wbd_ref[...], preferred_element_type=F32) * scale
        mix_scr[...] = mix.astype(BF16)

    kt = FFT_K1_TILE
    rows = FFT_N2 * kt
    zs = []
    for t in range(FFT_K1_STEP // kt):
        sl = slice(kt * t, kt * (t + 1))
        rhs = jnp.concatenate(
            [re_ref[0, :, sl, :].reshape(rows, 512), mi_ref[0, :, sl, :].reshape(rows, 512)],
            axis=0,
        )
        z = jnp.dot(k3_ref[...], rhs, preferred_element_type=F32)
        zs.append(jnp.concatenate([z[:rows], z[rows:]], axis=1).astype(BF16))
    y = jnp.dot(jnp.concatenate(zs, axis=0), mix_scr[...], preferred_element_type=F32)
    for t in range(FFT_K1_STEP // kt):
        sl = slice(kt * t, kt * (t + 1))
        zf = zf_ref[0, :, sl, :].reshape(rows, 512).astype(F32)
        yt = y[rows * t : rows * (t + 1)] * _silu(zf)
        y_ref[0, :, sl, :] = yt.astype(BF16).reshape(FFT_N2, kt, 512)


def _dft_cos_sin(n):
    j = np.arange(n, dtype=np.int64)
    ang = 2.0 * np.pi * ((j[:, None] * j[None, :]) % n).astype(np.float64) / n
    return np.cos(ang), np.sin(ang)


@functools.lru_cache(maxsize=None)
def _fft_constants(s):
    n2 = FFT_N2
    n1 = s // n2
    c1, s1 = _dft_cos_sin(n1)
    w1 = np.concatenate([c1, s1], axis=0)
    k1 = np.arange(n1, dtype=np.int64)
    s2 = np.arange(n2, dtype=np.int64)
    ang = 2.0 * np.pi * ((s2[:, None] * k1[None, :]) % s).astype(np.float64) / s
    tc = np.repeat(np.cos(ang)[:, :, None], LANES, axis=2)
    ts = np.repeat(np.sin(ang)[:, :, None], LANES, axis=2)
    c2, s2m = _dft_cos_sin(n2)
    eye = np.eye(FFT_K1_TILE)
    kc = np.kron(c2, eye)
    ks = np.kron(s2m, eye)
    k3 = np.block([[kc, -ks], [ks, kc]])
    cc, sc = _dft_cos_sin(F_GROUP)
    eye_g = np.eye(N_FGROUPS)
    cs = np.concatenate([np.kron(eye_g, cc), -np.kron(eye_g, sc)], axis=0)
    return (
        jnp.asarray(w1, BF16),
        jnp.asarray(tc, F32),
        jnp.asarray(ts, F32),
        jnp.asarray(k3, BF16),
        jnp.asarray(cs, BF16),
    )


def _fnet(u_f4, z_f, wbd_b):
    b, _, s, _ = u_f4.shape
    n2 = FFT_N2
    n1 = s // n2
    assert s == n1 * n2 and n1 % FFT_K1_STEP == 0, s
    w1, tc, ts, k3, cs = _fft_constants(s)
    a_shape = jax.ShapeDtypeStruct((b, n2, n1, 512), BF16)
    re, mi = pl.pallas_call(
        _fft1_kernel,
        grid=(b, 512 // (FFT_SLABS * LANES)),
        in_specs=[
            pl.BlockSpec((1, FFT_SLABS, s, LANES), lambda i, j: (i, j, 0, 0)),
            pl.BlockSpec((2 * n1, n1), lambda i, j: (0, 0)),
            pl.BlockSpec((n2, n1, LANES), lambda i, j: (0, 0, 0)),
            pl.BlockSpec((n2, n1, LANES), lambda i, j: (0, 0, 0)),
        ],
        out_specs=[
            pl.BlockSpec((1, n2, n1, FFT_SLABS * LANES), lambda i, j: (i, 0, 0, j)),
            pl.BlockSpec((1, n2, n1, FFT_SLABS * LANES), lambda i, j: (i, 0, 0, j)),
        ],
        out_shape=[a_shape, a_shape],
        compiler_params=_cparams(2),
        name="fft1",
    )(u_f4, w1, tc, ts)

    blk = pl.BlockSpec((1, n2, FFT_K1_STEP, 512), lambda i, j: (i, 0, j, 0))
    fixed = lambda i, j: (0, 0)
    scale = 1.0 / math.sqrt(float(s) * F_GROUP)
    y4 = pl.pallas_call(
        functools.partial(_fft3_kernel, scale=scale),
        grid=(b, n1 // FFT_K1_STEP),
        in_specs=[
            blk,
            blk,
            pl.BlockSpec(k3.shape, fixed),
            pl.BlockSpec(cs.shape, fixed),
            pl.BlockSpec((512, 512), fixed),
            blk,
        ],
        out_specs=blk,
        out_shape=jax.ShapeDtypeStruct((b, n2, n1, 512), BF16),
        scratch_shapes=[pltpu.VMEM(cs.shape, BF16)],
        compiler_params=_cparams(2),
        name="fft3",
    )(re, mi, k3, cs, wbd_b, z_f.reshape(b, n2, n1, 512))
    return y4.reshape(b, s, 512)


def _split3_dot(tri, v):
    v1 = v.astype(BF16)
    r1 = v - v1.astype(F32)
    v2 = r1.astype(BF16)
    v3 = (r1 - v2.astype(F32)).astype(BF16)
    d = lambda w: jnp.dot(tri, w, preferred_element_type=F32)
    return d(v1) + d(v2) + d(v3)


N_PANELS = 4
LOG2E = math.log2(math.e)


def _ssd_panels(dt, alog, tri, pan_ref, etot_ref, c, *, forward):
    a = dt * (-jnp.exp(alog))
    acs = _split3_dot(tri, a)
    total = acs[CHUNK - 1 : CHUNK, :] if forward else acs[0:1, :]
    acs2 = acs * LOG2E
    l2dt = jnp.log2(dt)
    pan_ref[c, 0] = acs2
    pan_ref[c, 1] = (acs2 - l2dt).T
    pan_ref[c, 2] = l2dt.T
    pan_ref[c, 3] = (dt * jnp.exp(total - acs)).T
    etot_ref[c] = jnp.broadcast_to(jnp.exp(total), (8, LANES))


def _zero_after(v):
    bits = lax.bitcast_convert_type(v[0:8, 0:LANES], jnp.uint32)
    return ((bits >> 16) >> 16)[0:1, :].astype(F32)


def _ssd_chunk(act_ref, r0, pan_ref, etot_ref, ci, state_ref, gb_ref, emit, *, lane0, forward,
               after=None):
    rows = pl.ds(r0, CHUNK)
    e_tot = etot_ref[ci][0:1, :]
    if after is not None:
        e_tot = e_tot + after

    li = lax.broadcasted_iota(jnp.int32, (CHUNK, CHUNK), 0)
    si = lax.broadcasted_iota(jnp.int32, (CHUNK, CHUNK), 1)
    keep = (li >= si) if forward else (li <= si)
    left = lax.broadcasted_iota(jnp.int32, (CHUNK, LANES), 1) < HEAD_DIM
    left_row = lax.broadcasted_iota(jnp.int32, (1, LANES), 1) < HEAD_DIM
    nt_dims = (((1,), (1,)), ((), ()))

    for g in range(N_BC_GROUPS):
        bg = act_ref[0, rows, pl.ds(D_SSM + D_STATE * g, D_STATE)]
        cg = act_ref[0, rows, pl.ds(D_SSM + 512 + D_STATE * g, D_STATE)]
        cb = lax.dot_general(cg, bg, nt_dims, preferred_element_type=F32)
        gb_ref[0] = jnp.where(keep, cb, 0.0).astype(BF16)
        gb_ref[1] = bg.astype(F32).T.astype(BF16)
        y_off = jnp.dot(cg, state_ref[g].astype(BF16), preferred_element_type=F32)
        ys, sts = [], []
        for j in range(HEADS_PER_GROUP // 2):
            pair = (HEADS_PER_GROUP // 2) * g + j
            psl = slice(LANES * j, LANES * (j + 1))
            xp = act_ref[0, rows, pl.ds(LANES * pair, LANES)]
            zero = jnp.zeros_like(xp)
            xbd = jnp.concatenate([jnp.where(left, xp, zero), jnp.where(left, zero, xp)], axis=0)
            gl, wl, es, et = [], [], [], []
            for h in (2 * pair, 2 * pair + 1):
                hl = lane0 + h
                row = lambda p: pan_ref[ci, p, pl.ds(hl, 1), :]
                colb = jnp.broadcast_to(pan_ref[ci, 0][:, hl : hl + 1], (CHUNK, LANES))
                dec = jnp.exp2(jnp.minimum(colb - row(1), row(2)))
                gl.append(dec.astype(BF16) * gb_ref[0])
                wl.append(gb_ref[1] * row(3).astype(BF16))
                es.append(colb)
                et.append(jnp.broadcast_to(e_tot[:, hl : hl + 1], (1, LANES)))
            lhs = jnp.concatenate(
                [jnp.concatenate(gl, axis=1), jnp.concatenate(wl, axis=1)], axis=0
            )
            res = jnp.dot(lhs, xbd, preferred_element_type=F32)
            ys.append(res[:CHUNK] + y_off[:, psl] * jnp.exp2(jnp.where(left, es[0], es[1])))
            sts.append(
                state_ref[g, :, psl] * jnp.where(left_row, et[0], et[1]) + res[CHUNK:]
            )
        state_ref[g] = jnp.concatenate(sts, axis=1)
        emit(g, jnp.concatenate(ys, axis=1))


def _ssd_fwd_kernel(act_ref, dt_ref, alog_ref, tri_ref, yf_ref, state_scr, pan_scr, etot_scr,
                    gb_scr, *, tm):
    @pl.when(pl.program_id(1) == 0)
    def _():
        state_scr[...] = jnp.zeros_like(state_scr)

    for c in range(tm // CHUNK):
        _ssd_panels(dt_ref[0, CHUNK * c : CHUNK * (c + 1), :], alog_ref[...], tri_ref[...],
                    pan_scr, etot_scr, c, forward=True)

    def body(ci, carry):
        r0 = pl.multiple_of(ci * CHUNK, CHUNK)

        def emit(g, y):
            yf_ref[0, pl.ds(r0, CHUNK), pl.ds(GROUP_W * g, GROUP_W)] = y

        _ssd_chunk(act_ref, r0, pan_scr, etot_scr, ci, state_scr, gb_scr, emit, lane0=0,
                   forward=True)
        return carry

    lax.fori_loop(0, tm // CHUNK, body, 0, unroll=SSD_FWD_UNROLL)


@functools.lru_cache(maxsize=None)
def _ssd_constants():
    tri_f = np.tril(np.ones((CHUNK, CHUNK)))
    tri_b = np.triu(np.ones((CHUNK, CHUNK)))
    return jnp.asarray(tri_f, BF16), jnp.asarray(tri_b, BF16)


def _ssd_fwd(act, dt, alog_p, tm):
    b, s, _ = act.shape
    nt = s // tm
    tri_f, _ = _ssd_constants()
    fixed = lambda bi, i: (0, 0)
    tile = lambda w: pl.BlockSpec((1, tm, w), lambda bi, i: (bi, i, 0))
    scratch = [
        pltpu.VMEM((N_BC_GROUPS, D_STATE, GROUP_W), F32),
        pltpu.VMEM((tm // CHUNK, N_PANELS, CHUNK, LANES), F32),
        pltpu.VMEM((tm // CHUNK, 8, LANES), F32),
        pltpu.VMEM((2, CHUNK, CHUNK), BF16),
    ]

    return pl.pallas_call(
        functools.partial(_ssd_fwd_kernel, tm=tm),
        grid=(b, nt),
        in_specs=[
            tile(CONV_DIM),
            tile(LANES),
            pl.BlockSpec((1, LANES), fixed),
            pl.BlockSpec((CHUNK, CHUNK), fixed),
        ],
        out_specs=tile(D_SSM),
        out_shape=jax.ShapeDtypeStruct((b, s, D_SSM), F32),
        scratch_shapes=scratch,
        compiler_params=_cparams(2),
        name="ssd_fwd",
    )(act, dt, alog_p, tri_f)


def _tail_kernel(act_ref, dt_ref, yf_ref, zs_ref, x_ref, yfn_ref, p_ref, alog_ref, tri_ref,
                 dskip_ref, nw_ref, wo_ref, wg_ref, wp_ref, fw_ref, o_ref, state_scr, pan_scr,
                 etot_scr, gb_scr, ys_scr, ysp_scr, *, tm, nt, ntot):
    q = pl.program_id(0)

    @pl.when(q == 0)
    def _():
        ys_scr[...] = jnp.zeros_like(ys_scr)

    ysp_scr[...] = ys_scr[...]

    @pl.when(jnp.minimum(q, ntot - 1) % nt == 0)
    def _():
        state_scr[...] = jnp.zeros_like(state_scr)

    nc = tm // CHUNK
    for c in range(nc):
        _ssd_panels(dt_ref[0, CHUNK * c : CHUNK * (c + 1), :], alog_ref[...], tri_ref[...],
                    pan_scr, etot_scr, c, forward=False)

    mix = jnp.concatenate([yfn_ref[0], ysp_scr[...]], axis=1)
    ncol = D_MODEL // nc
    h_blocks = []

    for ci in reversed(range(nc)):
        cs = slice(ncol * len(h_blocks), ncol * (len(h_blocks) + 1))
        h_blocks.append(
            x_ref[0, :, cs] + jnp.dot(mix, wo_ref[:, cs], preferred_element_type=F32)
        )
        rows = slice(CHUNK * ci, CHUNK * (ci + 1))

        def emit(g, y, rows=rows):
            gsl = slice(GROUP_W * g, GROUP_W * (g + 1))
            xs = act_ref[0, rows, gsl].astype(F32)
            y = y + yf_ref[0, rows, gsl] + dskip_ref[:, gsl] * xs
            y = y * zs_ref[0, rows, gsl].astype(F32)
            ms = jnp.mean(y * y, axis=-1, keepdims=True)
            ys_scr[rows, gsl] = ((y * lax.rsqrt(ms + EPS)) * nw_ref[:, gsl]).astype(BF16)

        _ssd_chunk(act_ref, CHUNK * ci, pan_scr, etot_scr, ci, state_scr, gb_scr, emit,
                   lane0=N_HEADS, forward=False, after=_zero_after(h_blocks[-1]))

    h = jnp.concatenate(h_blocks, axis=1)
    gate = _sigmoid(jnp.dot(h.astype(BF16), wg_ref[...], preferred_element_type=F32))
    ple = jnp.dot(p_ref[0].astype(BF16), wp_ref[...], preferred_element_type=F32)
    h = h + ple * gate
    ms = jnp.mean(h * h, axis=-1, keepdims=True)
    o_ref[0] = (h * lax.rsqrt(ms + EPS)) * fw_ref[...]


def _tail(act, dt, y_fwd, z_s, x, y_f, p, alog_p, dskip_x, ssd_nw, w_out_b, w_gate_b, w_ple_b,
          final_w, tm):
    b, s, _ = act.shape
    nt = s // tm
    ntot = b * nt
    _, tri_b = _ssd_constants()

    def scan_tile(q):
        qs = jnp.minimum(q, ntot - 1)
        return qs // nt, nt - 1 - qs % nt, 0

    def epi_tile(q):
        return scan_tile(jnp.maximum(q - 1, 0))

    stile = lambda w: pl.BlockSpec((1, tm, w), scan_tile)
    etile = lambda w: pl.BlockSpec((1, tm, w), epi_tile)
    fixed = lambda shape: pl.BlockSpec(shape, lambda q: (0, 0))
    return pl.pallas_call(
        functools.partial(_tail_kernel, tm=tm, nt=nt, ntot=ntot),
        grid=(ntot + 1,),
        in_specs=[
            stile(CONV_DIM), stile(LANES), stile(D_SSM), stile(D_SSM),
            etile(D_MODEL), etile(512), etile(D_PLE),
            fixed((1, LANES)), fixed((CHUNK, CHUNK)), fixed((1, D_SSM)), fixed((1, D_SSM)),
            fixed((2048, D_MODEL)), fixed((D_MODEL, D_MODEL)), fixed((D_PLE, D_MODEL)),
            fixed((1, D_MODEL)),
        ],
        out_specs=etile(D_MODEL),
        out_shape=jax.ShapeDtypeStruct((b, s, D_MODEL), F32),
        scratch_shapes=[
            pltpu.VMEM((N_BC_GROUPS, D_STATE, GROUP_W), F32),
            pltpu.VMEM((tm // CHUNK, N_PANELS, CHUNK, LANES), F32),
            pltpu.VMEM((tm // CHUNK, 8, LANES), F32),
            pltpu.VMEM((2, CHUNK, CHUNK), BF16),
            pltpu.VMEM((tm, D_SSM), BF16),
            pltpu.VMEM((tm, D_SSM), BF16),
        ],
        compiler_params=_cparams(1),
        name="tail",
    )(act, dt, y_fwd, z_s, x, y_f, p, alog_p, tri_b, dskip_x, ssd_nw, w_out_b, w_gate_b,
      w_ple_b, final_w)


def _trunk(x, p, norm_w, w_in, w_fmix, conv_w, conv_b, a_log_f, a_log_b, dt_bias_f,
           dt_bias_b, d_skip, ssd_norm_w, w_out, w_ple_in, w_ple_gate, final_norm_w):
    b, s, _ = x.shape
    t = b * s
    tm = 512
    assert s % tm == 0, s

    pad = jnp.zeros((LANES - 2 * N_HEADS,), F32)
    bias_p = jnp.concatenate([dt_bias_f[0], dt_bias_b[0], pad])[None, :]
    alog_p = jnp.concatenate([a_log_f[0], a_log_b[0], pad])[None, :]
    dskip_x = jnp.repeat(d_skip[0], HEAD_DIM)[None, :]

    w_in_b = jnp.pad(w_in[0].astype(BF16), ((0, 0), (0, D_IN_PAD - D_IN_PROJ)))
    z_f, u_f, z_s, act, dt = _in_proj(
        x, norm_w[0][None, :], w_in_b, conv_w[0], conv_b[0][None, :], bias_p, tm
    )

    wbd = jax.scipy.linalg.block_diag(*[w_fmix[0, g] for g in range(N_FGROUPS)]).astype(BF16)
    y_f = _fnet(u_f, z_f, wbd)

    y_fwd = _ssd_fwd(act, dt, alog_p, 2 * tm)
    return _tail(
        act, dt, y_fwd, z_s, x, y_f, p[0], alog_p, dskip_x, ssd_norm_w[0][None, :],
        w_out[0].astype(BF16), w_ple_gate[0].astype(BF16), w_ple_in[0].astype(BF16),
        final_norm_w[None, :], tm,
    )


def kernel(x_prompt, x_sample, p_prompt, p_sample, norm_w, w_in, w_fmix, conv_w, conv_b, a_log_f, a_log_b, dt_bias_f, dt_bias_b, d_skip, ssd_norm_w, w_out, w_ple_in, w_ple_gate, final_norm_w):
    weights = (norm_w, w_in, w_fmix, conv_w, conv_b, a_log_f, a_log_b, dt_bias_f, dt_bias_b,
               d_skip, ssd_norm_w, w_out, w_ple_in, w_ple_gate, final_norm_w)
    return (_trunk(x_prompt, p_prompt, *weights), _trunk(x_sample, p_sample, *weights))
```
